```python
import jax, jax.numpy as jnp
from jax import lax
import numpy as np

D_MODEL = 1024
BATCH = 32
SEQ = 2048
DEPTH = 1

D_PLE = 256
HEAD_DIM = 64
ROPE_DIM = HEAD_DIM // 4
ROPE_THETA = 500000.0
N_HEADS_A = 8
N_HEADS_B = 8
WIDTH_A = N_HEADS_A * HEAD_DIM
WIDTH_B = N_HEADS_B * HEAD_DIM
N_IDX_HEADS = 8
IDX_DIM = 64
TOPK_MAX = 256
Q_BLOCK = 128
EPS = 1e-6
NEG = -1e30

SPLITS = (WIDTH_A, HEAD_DIM, HEAD_DIM, WIDTH_A,
          N_IDX_HEADS * IDX_DIM, IDX_DIM, N_IDX_HEADS,
          WIDTH_B, WIDTH_B, WIDTH_B, N_HEADS_B, WIDTH_B)
D_IN = sum(SPLITS)

kernel_name = "hybrid_dsa_fox_gated_block"


def _offsets():
    offs, acc = [], 0
    for w in SPLITS[:-1]:
        acc += w
        offs.append(acc)
    return offs


def rmsnorm(x, g):
    xf = x.astype(jnp.float32)
    y = xf * lax.rsqrt(jnp.mean(xf * xf, axis=-1, keepdims=True) + EPS) * g.astype(jnp.float32)
    return y.astype(x.dtype)


def partial_rope(x, pos):
    half = ROPE_DIM // 2
    freqs = ROPE_THETA ** (-jnp.arange(half, dtype=jnp.float32) / half)
    ang = pos.astype(jnp.float32)[:, None] * freqs[None, :]
    cos = jnp.cos(ang)[None, :, None, :]
    sin = jnp.sin(ang)[None, :, None, :]
    x1 = x[..., :half].astype(jnp.float32)
    x2 = x[..., half:ROPE_DIM].astype(jnp.float32)
    rot = jnp.concatenate([x1 * cos - x2 * sin, x2 * cos + x1 * sin], axis=-1).astype(x.dtype)
    return jnp.concatenate([rot, x[..., ROPE_DIM:]], axis=-1)


def to_blocks(t):
    B, S = t.shape[:2]
    t = t.reshape((B, S // Q_BLOCK, Q_BLOCK) + t.shape[2:])
    return jnp.moveaxis(t, 1, 0)


def from_blocks(t):
    t = jnp.moveaxis(t, 0, 1)
    return t.reshape((t.shape[0], t.shape[1] * t.shape[2]) + t.shape[3:])


def dsa_branch(qa, ka, va, qi, ki, wi):
    B, S = ka.shape[:2]
    topk = min(TOPK_MAX, S // 4)
    nb = S // Q_BLOCK
    key_pos = jnp.arange(S)
    scale = HEAD_DIM ** -0.5

    def block(args):
        blk, q_blk, qi_blk, wi_blk = args
        q_pos = blk * Q_BLOCK + jnp.arange(Q_BLOCK)
        causal = key_pos[None, :] <= q_pos[:, None]
        dots = jnp.einsum('bqhd,bsd->bqhs', qi_blk, ki)
        score = jnp.einsum('bqhs,bqh->bqs', jax.nn.relu(dots).astype(jnp.float32),
                           wi_blk.astype(jnp.float32))
        score = jnp.where(causal[None], score, -jnp.inf)
        _, sel = lax.top_k(score, topk)
        k_sel = jax.vmap(lambda k, i: k[i])(ka, sel)
        v_sel = jax.vmap(lambda v, i: v[i])(va, sel)
        logits = jnp.einsum('bqhd,bqkd->bhqk', q_blk, k_sel).astype(jnp.float32) * scale
        valid = sel <= q_pos[None, :, None]
        logits = jnp.where(valid[:, None], logits, NEG)
        probs = jax.nn.softmax(logits, axis=-1).astype(v_sel.dtype)
        return jnp.einsum('bhqk,bqkd->bqhd', probs, v_sel)

    out = lax.map(block, (jnp.arange(nb), to_blocks(qa), to_blocks(qi), to_blocks(wi)))
    return from_blocks(out)


def fox_branch(qb, kb, vb, log_f):
    B, S = qb.shape[:2]
    nb = S // Q_BLOCK
    key_pos = jnp.arange(S)
    scale = HEAD_DIM ** -0.5
    c = jnp.cumsum(log_f, axis=1)
    c_k = jnp.transpose(c, (0, 2, 1))

    def block(args):
        blk, q_blk, cq_blk = args
        q_pos = blk * Q_BLOCK + jnp.arange(Q_BLOCK)
        causal = key_pos[None, :] <= q_pos[:, None]
        logits = jnp.einsum('bqhd,bshd->bhqs', q_blk, kb).astype(jnp.float32) * scale
        logits = logits + jnp.transpose(cq_blk, (0, 2, 1))[..., None] - c_k[:, :, None, :]
        logits = jnp.where(causal[None, None], logits, NEG)
        probs = jax.nn.softmax(logits, axis=-1).astype(vb.dtype)
        return jnp.einsum('bhqs,bshd->bqhd', probs, vb)

    out = lax.map(block, (jnp.arange(nb), to_blocks(qb), to_blocks(c)))
    return from_blocks(out)


def setup_inputs(seed: int = 0) -> dict:
    key = jax.random.key(seed)
    ks = jax.random.split(key, 14)
    f32 = jnp.float32
    nrm = lambda k, shape, fan_in: jax.random.normal(k, shape, f32) * (fan_in ** -0.5)
    x = jax.random.normal(ks[0], (BATCH, SEQ, D_MODEL), f32)
    p = jax.random.normal(ks[1], (DEPTH, BATCH, SEQ, D_PLE), f32)
    w_in = nrm(ks[2], (DEPTH, D_MODEL, D_IN), D_MODEL)
    b_forget = jax.random.uniform(ks[3], (DEPTH, N_HEADS_B), f32, 1.0, 5.0)
    w_branch_a = nrm(ks[4], (DEPTH, WIDTH_A, D_MODEL), WIDTH_A)
    w_branch_b = nrm(ks[5], (DEPTH, WIDTH_B, D_MODEL), WIDTH_B)
    w_merge = nrm(ks[6], (DEPTH, D_MODEL, 2 * D_MODEL), D_MODEL)
    w_out = nrm(ks[7], (DEPTH, D_MODEL, D_MODEL), D_MODEL)
    g_pre = 1.0 + 0.02 * jax.random.normal(ks[8], (DEPTH, D_MODEL), f32)
    g_post = 1.0 + 0.02 * jax.random.normal(ks[9], (DEPTH, D_MODEL), f32)
    w_ple = nrm(ks[10], (DEPTH, D_PLE, D_MODEL), D_PLE)
    w_ple_gate = nrm(ks[11], (DEPTH, D_MODEL, D_MODEL), D_MODEL)
    g_ple = 1.0 + 0.02 * jax.random.normal(ks[12], (DEPTH, D_MODEL), f32)
    return {"x": x, "p": p, "w_in": w_in, "b_forget": b_forget,
            "w_branch_a": w_branch_a, "w_branch_b": w_branch_b, "w_merge": w_merge,
            "w_out": w_out, "g_pre": g_pre, "g_post": g_post, "w_ple": w_ple,
            "w_ple_gate": w_ple_gate, "g_ple": g_ple}


def reference(x, p, w_in, b_forget, w_branch_a, w_branch_b, w_merge, w_out,
              g_pre, g_post, w_ple, w_ple_gate, g_ple):
    B, S, _ = x.shape
    pos = jnp.arange(S)
    offs = _offsets()
    idx_scale = (N_IDX_HEADS ** -0.5) * (IDX_DIM ** -0.5)
    for i in range(DEPTH):
        h = rmsnorm(x, g_pre[i])
        proj = h @ w_in[i]
        (qa, ka, va, gate_a, qi, ki, wi, qb, kb, vb, fb, gate_b) = jnp.split(proj, offs, axis=-1)

        qa = partial_rope(qa.reshape(B, S, N_HEADS_A, HEAD_DIM), pos)
        ka = partial_rope(ka.reshape(B, S, 1, HEAD_DIM), pos).reshape(B, S, HEAD_DIM)
        qi = partial_rope(qi.reshape(B, S, N_IDX_HEADS, IDX_DIM), pos)
        ki = partial_rope(ki.reshape(B, S, 1, IDX_DIM), pos).reshape(B, S, IDX_DIM)
        wi = wi * idx_scale
        att_a = dsa_branch(qa, ka, va, qi, ki, wi).reshape(B, S, WIDTH_A)
        y_a = (att_a * jax.nn.silu(gate_a)) @ w_branch_a[i]

        log_f = jax.nn.log_sigmoid(fb.astype(jnp.float32) + b_forget[i].astype(jnp.float32))
        att_b = fox_branch(qb.reshape(B, S, N_HEADS_B, HEAD_DIM),
                           kb.reshape(B, S, N_HEADS_B, HEAD_DIM),
                           vb.reshape(B, S, N_HEADS_B, HEAD_DIM), log_f).reshape(B, S, WIDTH_B)
        y_b = (att_b * jax.nn.silu(gate_b)) @ w_branch_b[i]

        m_a, m_b = jnp.split(jax.nn.sigmoid(h @ w_merge[i]), 2, axis=-1)
        out = (m_a * y_a + m_b * y_b) @ w_out[i]
        x = x + rmsnorm(out, g_post[i])

        e = p[i] @ w_ple[i]
        gate = jax.nn.sigmoid(x @ w_ple_gate[i])
        x = x + rmsnorm(gate * e, g_ple[i])
    return x
```

```python
import functools

import numpy as np
import jax
import jax.numpy as jnp
from jax import lax
from jax.experimental import pallas as pl
from jax.experimental.pallas import tpu as pltpu

HEAD_DIM = 64
ROPE_DIM = HEAD_DIM // 4
ROPE_THETA = 500000.0
N_HEADS = 8
WIDTH = N_HEADS * HEAD_DIM
TOPK_MAX = 256
EPS = 1e-6
NEG = -1e30

LANES = 128
SMALL_W = 2 * LANES
WI_LANE = HEAD_DIM
FB_LANE = HEAD_DIM + N_HEADS
INT_MIN = np.int32(-2**31)
VMEM_LIMIT = 56 * 1024 * 1024

F32 = jnp.float32
BF16 = jnp.bfloat16
NT = (((1,), (1,)), ((), ()))


def _rmsnorm(x, g):
    return x * lax.rsqrt(jnp.mean(x * x, axis=-1, keepdims=True) + EPS) * g


def _proj_kernel(x_ref, g_ref, wbig_ref, wsmall_ref, rc_ref, rs1_ref, rs2_ref,
                 qa_ref, ga_ref, qi_ref, qb_ref, kb_ref, vb_ref, gb_ref, small_ref,
                 *, idx_scale, q_scale):
    hb = _rmsnorm(x_ref[...], g_ref[...]).astype(BF16)
    rc, rs1, rs2 = rc_ref[...], rs1_ref[...], rs2_ref[...]

    def rope(y, c, s1, s2):
        return y * c + pltpu.roll(y, LANES - ROPE_DIM // 2, 1) * s1 + pltpu.roll(y, ROPE_DIM // 2, 1) * s2

    def proj(j):
        return jnp.dot(hb, wbig_ref[:, j * WIDTH:(j + 1) * WIDTH], preferred_element_type=F32)

    y = proj(0)
    for c in range(WIDTH // LANES):
        sl = slice(c * LANES, (c + 1) * LANES)
        qa_ref[:, sl] = (rope(y[:, sl], rc, rs1, rs2) * q_scale).astype(BF16)
    ga_ref[...] = proj(1).astype(BF16)
    y = proj(2)
    for c in range(WIDTH // LANES):
        sl = slice(c * LANES, (c + 1) * LANES)
        qi_ref[:, sl] = rope(y[:, sl], rc, rs1, rs2).astype(BF16)
    qb_ref[...] = (proj(3) * q_scale).astype(BF16)
    kb_ref[...] = proj(4).astype(BF16)
    vb_ref[...] = proj(5).astype(BF16)
    gb_ref[...] = proj(6).astype(BF16)

    ys = jnp.dot(hb, wsmall_ref[...], preferred_element_type=F32)
    lane = lax.broadcasted_iota(jnp.int32, rc.shape, 1)
    lo = lane < HEAD_DIM
    c_lo = jnp.where(lo, rc, 1.0)
    s1_lo = jnp.where(lo, rs1, 0.0)
    s2_lo = jnp.where(lo, rs2, 0.0)
    small_ref[:, 0:LANES] = rope(ys[:, 0:LANES], c_lo, s1_lo, s2_lo)
    wi_scale = jnp.where(lane < WI_LANE, 1.0, jnp.where(lane < FB_LANE, idx_scale, 1.0))
    small_ref[:, LANES:SMALL_W] = rope(ys[:, LANES:SMALL_W], c_lo, s1_lo, s2_lo) * wi_scale


def _fox_kernel(q_ref, k_ref, v_ref, sm_ref, bf_ref, o_ref, ccol_ref, crow_ref, *, S, TQ):
    TK = TQ
    i = pl.program_id(1)

    @pl.when(i == 0)
    def _():
        r_io = lax.broadcasted_iota(jnp.int32, (LANES, LANES), 0)
        c_io = lax.broadcasted_iota(jnp.int32, (LANES, LANES), 1)
        tri = jnp.where(r_io >= c_io, 1.0, 0.0).astype(BF16)
        carry = jnp.zeros((1, LANES), F32)
        for r in range(S // LANES):
            z = sm_ref[0, r * LANES:(r + 1) * LANES, :] + bf_ref[...]
            lf = -(jnp.maximum(-z, 0.0) + jnp.log1p(jnp.exp(-jnp.abs(z))))
            a1 = lf.astype(BF16)
            r1 = lf - a1.astype(F32)
            a2 = r1.astype(BF16)
            a3 = (r1 - a2.astype(F32)).astype(BF16)
            cb = (jnp.dot(tri, a1, preferred_element_type=F32)
                  + jnp.dot(tri, a2, preferred_element_type=F32)
                  + jnp.dot(tri, a3, preferred_element_type=F32)) + carry
            ccol_ref[r * LANES:(r + 1) * LANES, :] = cb
            crow_ref[(r * LANES) // TK, :, (r * LANES) % TK:(r * LANES) % TK + LANES] = cb.T
            carry = cb[LANES - 1:LANES, :]

    q0 = pl.multiple_of(i * TQ, TQ)
    lane = lax.broadcasted_iota(jnp.int32, (TQ, LANES), 1)
    lane_k = lax.broadcasted_iota(jnp.int32, (TK, LANES), 1)
    row_io = lax.broadcasted_iota(jnp.int32, (TQ, TK), 0)
    col_io = lax.broadcasted_iota(jnp.int32, (TQ, TK), 1)

    for p in range(N_HEADS // 2):
        psl = slice(p * LANES, (p + 1) * LANES)
        q2 = q_ref[0, :, psl]
        outs = []
        for hh in range(2):
            h = 2 * p + hh
            own = (lane < HEAD_DIM) if hh == 0 else (lane >= HEAD_DIM)
            own_k = (lane_k < HEAD_DIM) if hh == 0 else (lane_k >= HEAD_DIM)
            qh = jnp.where(own, q2, jnp.zeros_like(q2))
            cq = ccol_ref[pl.ds(q0, TQ), FB_LANE + h:FB_LANE + h + 1]

            def chunk(kc, carry, masked, qh=qh, cq=cq, own_k=own_k, h=h, psl=psl):
                m, acc = carry
                k0 = pl.multiple_of(kc * TK, TK)
                k2 = k_ref[0, pl.ds(k0, TK), psl]
                v2 = v_ref[0, pl.ds(k0, TK), psl]
                vx = jnp.where(own_k, v2, jnp.ones_like(v2))
                s = lax.dot_general(qh, k2, NT, preferred_element_type=F32)
                t = s - crow_ref[kc, FB_LANE + h:FB_LANE + h + 1, :]
                if masked:
                    t = jnp.where(col_io <= row_io, t, NEG)
                m_new = jnp.maximum(m, jnp.max(t, axis=1, keepdims=True) + cq)
                alpha = jnp.exp(m - m_new)
                pm = jnp.exp(t + (cq - m_new)).astype(BF16)
                acc = alpha * acc + jnp.dot(pm, vx, preferred_element_type=F32)
                return m_new, acc

            init = (jnp.full((TQ, 1), -jnp.inf, F32), jnp.zeros((TQ, LANES), F32))
            carry = lax.fori_loop(0, i, functools.partial(chunk, masked=False), init)
            _, acc = chunk(i, carry, True)
            outs.append(acc / pltpu.roll(acc, HEAD_DIM, 1))
        o_ref[0, :, psl] = jnp.where(lane < HEAD_DIM, outs[0], outs[1]).astype(BF16)


def _dsa_kernel(qi_ref, qa_ref, sm_ref, o_ref,
                ki2_ref, ka2_ref, vtlo_ref, vthi_ref, keys_ref, bias_ref, *, S, TQ, K):
    TK = TQ
    i = pl.program_id(1)
    nk = i + 1

    @pl.when(i == 0)
    def _():
        lane = lax.broadcasted_iota(jnp.int32, (LANES, LANES), 1)
        lo = lane < HEAD_DIM
        for r in range(S // LANES):
            c0 = sm_ref[0, r * LANES:(r + 1) * LANES, 0:LANES]
            c1 = sm_ref[0, r * LANES:(r + 1) * LANES, LANES:SMALL_W]
            c0r = pltpu.roll(c0, HEAD_DIM, 1)
            kc, off = (r * LANES) // TK, (r * LANES) % TK
            ka2_ref[kc, off:off + LANES, :] = jnp.where(lo, c0, c0r).astype(BF16)
            ki2_ref[kc, off:off + LANES, :] = jnp.where(lo, c1, pltpu.roll(c1, HEAD_DIM, 1)).astype(BF16)
            vtlo_ref[kc, :, off:off + LANES] = jnp.where(lo, c0r, 1.0).T.astype(BF16)
            vthi_ref[kc, :, off:off + LANES] = jnp.where(lo, 1.0, c0).T.astype(BF16)

    q0 = pl.multiple_of(i * TQ, TQ)
    lane_q = lax.broadcasted_iota(jnp.int32, (TQ, LANES), 1)
    kpos_io = lax.broadcasted_iota(jnp.int32, (TK, TQ), 0)
    qpos = q0 + lax.broadcasted_iota(jnp.int32, (TK, TQ), 1)

    def head_q(ref, h):
        q2 = ref[0, :, (h // 2) * LANES:(h // 2 + 1) * LANES]
        own = (lane_q < HEAD_DIM) if h % 2 == 0 else (lane_q >= HEAD_DIM)
        return jnp.where(own, q2, jnp.zeros_like(q2))

    w_t = sm_ref[0, pl.ds(q0, TQ), LANES:SMALL_W].T
    qi_h = [head_q(qi_ref, h) for h in range(N_HEADS)]

    def score_chunk(kc, _):
        kch = ki2_ref[kc]
        acc = jnp.zeros((TK, TQ), F32)
        for h in range(N_HEADS):
            d = lax.dot_general(kch, qi_h[h], NT, preferred_element_type=F32)
            acc = acc + w_t[WI_LANE + h:WI_LANE + h + 1, :] * jnp.maximum(d, 0.0)
        bits = lax.bitcast_convert_type(acc, jnp.int32)
        key = bits ^ ((bits >> 31) & np.int32(0x7FFFFFFF))
        key = jnp.where(kc * TK + kpos_io <= qpos, key, INT_MIN)
        keys_ref[kc] = key.reshape(TK // 8, 8, TQ)
        return 0

    lax.fori_loop(0, nk, score_chunk, 0)

    def count(pred_fn):
        def body(kc, c8):
            return c8 + jnp.sum(jnp.where(pred_fn(keys_ref[kc]), 1.0, 0.0), axis=0)
        c8 = lax.fori_loop(0, nk, body, jnp.zeros((8, TQ), F32))
        return jnp.sum(c8, axis=0, keepdims=True)

    def bit_body(it, res):
        cand = res | lax.shift_left(jnp.int32(1), 31 - it)
        cs8 = jnp.broadcast_to(cand ^ INT_MIN, (8, TQ))[None]
        cnt = count(lambda kk: kk >= cs8)
        return jnp.where(cnt >= K, cand, res)

    res = lax.fori_loop(0, 32, bit_body, jnp.zeros((1, TQ), jnp.int32))
    thr = res ^ INT_MIN
    thr8 = jnp.broadcast_to(thr, (8, TQ))[None]
    need = K - count(lambda kk: kk > thr8)

    r_io = lax.broadcasted_iota(jnp.int32, (TK, TK), 0)
    c_io = lax.broadcasted_iota(jnp.int32, (TK, TK), 1)
    tri = jnp.where(r_io >= c_io, 1.0, 0.0).astype(BF16)

    def sel_chunk(kc, seen):
        kk = keys_ref[kc].reshape(TK, TQ)
        eq = kk == thr
        pre = jnp.dot(tri, jnp.where(eq, 1.0, 0.0).astype(BF16), preferred_element_type=F32) + seen
        tie = jnp.where(eq, jnp.where(pre <= need, 0.0, NEG), NEG)
        b = jnp.where(kk > thr, 0.0, tie)
        bias_ref[kc] = jnp.where(kc * TK + kpos_io <= qpos, b, NEG)
        return pre[TK - 1:TK, :]

    lax.fori_loop(0, nk, sel_chunk, jnp.zeros((1, TQ), F32))

    row_o = lax.broadcasted_iota(jnp.int32, (LANES, TQ), 0)
    for p in range(N_HEADS // 2):
        halves = []
        for hh in range(2):
            qh = head_q(qa_ref, 2 * p + hh)
            vt_ref = vtlo_ref if hh == 0 else vthi_ref

            def att_chunk(kc, carry, qh=qh, vt_ref=vt_ref):
                m, acc = carry
                s = lax.dot_general(ka2_ref[kc], qh, NT, preferred_element_type=F32)
                t = s + bias_ref[kc]
                m_new = jnp.maximum(m, jnp.max(t, axis=0, keepdims=True))
                alpha = jnp.exp(m - m_new)
                pm = jnp.exp(t - m_new).astype(BF16)
                acc = alpha * acc + jnp.dot(vt_ref[kc], pm, preferred_element_type=F32)
                return m_new, acc

            init = (jnp.full((1, TQ), -jnp.inf, F32), jnp.zeros((LANES, TQ), F32))
            _, acc = lax.fori_loop(0, nk, att_chunk, init)
            l_row = acc[HEAD_DIM:HEAD_DIM + 1, :] if hh == 0 else acc[0:1, :]
            halves.append(acc / l_row)
        o_t = jnp.where(row_o < HEAD_DIM, halves[0], halves[1])
        o_ref[0, :, p * LANES:(p + 1) * LANES] = o_t.T.astype(BF16)


def _out_kernel(x_ref, aa_ref, ga_ref, ab_ref, gb_ref, p_ref, gpre_ref, gpost_ref, gple_ref,
                wm_ref, wa_ref, wb_ref, wo_ref, wp_ref, wg_ref, o_ref, *, D):
    x = x_ref[...]
    hb = _rmsnorm(x, gpre_ref[...]).astype(BF16)
    mg = jax.nn.sigmoid(jnp.dot(hb, wm_ref[...], preferred_element_type=F32))

    def branch(a_ref, g_ref, w_ref):
        g = g_ref[...].astype(F32)
        u = (a_ref[...].astype(F32) * (g * jax.nn.sigmoid(g))).astype(BF16)
        return jnp.dot(u, w_ref[...], preferred_element_type=F32)

    merged = mg[:, :D] * branch(aa_ref, ga_ref, wa_ref) + mg[:, D:] * branch(ab_ref, gb_ref, wb_ref)
    out = jnp.dot(merged.astype(BF16), wo_ref[...], preferred_element_type=F32)
    x1 = x + _rmsnorm(out, gpost_ref[...])
    e = jnp.dot(p_ref[...].astype(BF16), wp_ref[...], preferred_element_type=F32)
    gate = jax.nn.sigmoid(jnp.dot(x1.astype(BF16), wg_ref[...], preferred_element_type=F32))
    o_ref[...] = x1 + _rmsnorm(gate * e, gple_ref[...])


def _const_spec(shape):
    return pl.BlockSpec(shape, lambda *_: (0,) * len(shape), pipeline_mode=pl.Buffered(1))


def _rope_tables(S):
    half = ROPE_DIM // 2
    freqs = ROPE_THETA ** (-jnp.arange(half, dtype=F32) / half)
    ang = jnp.arange(S).astype(F32)[:, None] * freqs[None, :]
    cos, sin = jnp.cos(ang), jnp.sin(ang)
    pad = HEAD_DIM - ROPE_DIM
    c64 = jnp.concatenate([cos, cos, jnp.ones((S, pad), F32)], axis=1)
    s1 = jnp.concatenate([-sin, jnp.zeros((S, HEAD_DIM - half), F32)], axis=1)
    s2 = jnp.concatenate([jnp.zeros((S, half), F32), sin, jnp.zeros((S, pad), F32)], axis=1)
    rep = LANES // HEAD_DIM
    return jnp.tile(c64, (1, rep)), jnp.tile(s1, (1, rep)), jnp.tile(s2, (1, rep))


def _layer(x, p, w_in, b_forget, w_branch_a, w_branch_b, w_merge, w_out,
           g_pre, g_post, w_ple, w_ple_gate, g_ple):
    B, S, D = x.shape
    N = B * S
    d_ple = p.shape[-1]
    TM = 512
    TQ = 256
    assert S % TM == 0 and S % TQ == 0 and TQ % LANES == 0
    topk = min(TOPK_MAX, S // 4)
    idx_scale = (N_HEADS ** -0.5) * (HEAD_DIM ** -0.5)
    q_scale = HEAD_DIM ** -0.5

    W = WIDTH
    o_ka = W
    o_va = o_ka + HEAD_DIM
    o_ga = o_va + HEAD_DIM
    o_qi = o_ga + W
    o_ki = o_qi + W
    o_wi = o_ki + HEAD_DIM
    o_qb = o_wi + N_HEADS
    o_kb = o_qb + W
    o_vb = o_kb + W
    o_fb = o_vb + W
    o_gb = o_fb + N_HEADS
    assert o_gb + W == w_in.shape[1]
    cols = lambda o, n: w_in[:, o:o + n]
    w_big = jnp.concatenate([cols(0, W), cols(o_ga, W), cols(o_qi, W), cols(o_qb, W),
                             cols(o_kb, W), cols(o_vb, W), cols(o_gb, W)], axis=1).astype(BF16)
    w_small = jnp.concatenate([cols(o_ka, HEAD_DIM), cols(o_va, HEAD_DIM), cols(o_ki, HEAD_DIM),
                               cols(o_wi, N_HEADS), cols(o_fb, N_HEADS),
                               jnp.zeros((D, SMALL_W - 3 * HEAD_DIM - 2 * N_HEADS), w_in.dtype)],
                              axis=1).astype(BF16)
    rc, rs1, rs2 = _rope_tables(S)
    bf_pad = jnp.zeros((1, LANES), F32).at[0, FB_LANE:FB_LANE + N_HEADS].set(b_forget.astype(F32))

    params = functools.partial(pltpu.CompilerParams, vmem_limit_bytes=VMEM_LIMIT)
    x2 = x.reshape(N, D)
    tok = lambda w: pl.BlockSpec((TM, w), lambda i: (i, 0))
    rope_spec = pl.BlockSpec((TM, LANES), lambda i: (i % (S // TM), 0))
    act = jax.ShapeDtypeStruct((N, W), BF16)

    qa, ga, qi, qb, kb, vb, gb, small = pl.pallas_call(
        functools.partial(_proj_kernel, idx_scale=idx_scale, q_scale=q_scale),
        grid=(N // TM,),
        in_specs=[tok(D), _const_spec((1, D)), _const_spec((D, 7 * W)), _const_spec((D, SMALL_W)),
                  rope_spec, rope_spec, rope_spec],
        out_specs=[tok(W)] * 7 + [tok(SMALL_W)],
        out_shape=[act] * 7 + [jax.ShapeDtypeStruct((N, SMALL_W), F32)],
        compiler_params=params(dimension_semantics=("arbitrary",)),
        name="proj",
    )(x2, g_pre.reshape(1, D), w_big, w_small, rc, rs1, rs2)

    r3 = lambda a: a.reshape(B, S, a.shape[-1])
    small3 = r3(small)
    qblk = pl.BlockSpec((1, TQ, W), lambda b, i: (b, i, 0))
    seq = lambda w, j: pl.BlockSpec((1, S, w), lambda b, i: (b, 0, j))
    att = jax.ShapeDtypeStruct((B, S, W), BF16)

    att_b = pl.pallas_call(
        functools.partial(_fox_kernel, S=S, TQ=TQ),
        grid=(B, S // TQ),
        in_specs=[qblk, seq(W, 0), seq(W, 0), seq(LANES, 1),
                  pl.BlockSpec((1, LANES), lambda b, i: (0, 0))],
        out_specs=qblk,
        out_shape=att,
        scratch_shapes=[pltpu.VMEM((S, LANES), F32), pltpu.VMEM((S // TQ, LANES, TQ), F32)],
        compiler_params=params(dimension_semantics=("arbitrary", "arbitrary")),
        name="fox",
    )(r3(qb), r3(kb), r3(vb), small3, bf_pad)

    att_a = pl.pallas_call(
        functools.partial(_dsa_kernel, S=S, TQ=TQ, K=topk),
        grid=(B, S // TQ),
        in_specs=[qblk, qblk, seq(SMALL_W, 0)],
        out_specs=qblk,
        out_shape=att,
        scratch_shapes=[pltpu.VMEM((S // TQ, TQ, LANES), BF16), pltpu.VMEM((S // TQ, TQ, LANES), BF16),
                        pltpu.VMEM((S // TQ, LANES, TQ), BF16), pltpu.VMEM((S // TQ, LANES, TQ), BF16),
                        pltpu.VMEM((S // TQ, TQ // 8, 8, TQ), jnp.int32),
                        pltpu.VMEM((S // TQ, TQ, TQ), F32)],
        compiler_params=params(dimension_semantics=("arbitrary", "arbitrary")),
        name="dsa",
    )(r3(qi), r3(qa), small3)

    bf = lambda w: w.astype(BF16)
    vec = lambda g: g.reshape(1, D)
    return pl.pallas_call(
        functools.partial(_out_kernel, D=D),
        grid=(N // TM,),
        in_specs=[tok(D), tok(W), tok(W), tok(W), tok(W), tok(d_ple),
                  _const_spec((1, D)), _const_spec((1, D)), _const_spec((1, D)),
                  _const_spec((D, 2 * D)), _const_spec((W, D)), _const_spec((W, D)),
                  _const_spec((D, D)), _const_spec((d_ple, D)), _const_spec((D, D))],
        out_specs=tok(D),
        out_shape=jax.ShapeDtypeStruct((N, D), x.dtype),
        compiler_params=params(dimension_semantics=("arbitrary",)),
        name="out",
    )(x2, att_a.reshape(N, W), ga, att_b.reshape(N, W), gb, p.reshape(N, d_ple),
      vec(g_pre), vec(g_post), vec(g_ple),
      bf(w_merge), bf(w_branch_a), bf(w_branch_b), bf(w_out), bf(w_ple), bf(w_ple_gate)).reshape(B, S, D)


def kernel(x, p, w_in, b_forget, w_branch_a, w_branch_b, w_merge, w_out, g_pre, g_post, w_ple, w_ple_gate, g_ple):
    for i in range(p.shape[0]):
        x = _layer(x, p[i], w_in[i], b_forget[i], w_branch_a[i], w_branch_b[i], w_merge[i], w_out[i],
                   g_pre[i], g_post[i], w_ple[i], w_ple_gate[i], g_ple[i])
    return x
```

```python
import functools

import numpy as np
import jax
import jax.numpy as jnp
from jax import lax
from jax.experimental import pallas as pl
from jax.experimental.pallas import tpu as pltpu

HEAD_DIM = 64
ROPE_DIM = HEAD_DIM // 4
ROPE_THETA = 500000.0
N_HEADS = 8
WIDTH = N_HEADS * HEAD_DIM
TOPK_MAX = 256
EPS = 1e-6
NEG = -1e30

LANES = 128
SMALL_W = 2 * LANES
WI_LANE = HEAD_DIM
FB_LANE = HEAD_DIM + N_HEADS
INT_MIN = np.int32(-2**31)
VMEM_LIMIT = 56 * 1024 * 1024

F32 = jnp.float32
BF16 = jnp.bfloat16
NT = (((1,), (1,)), ((), ()))


def _rmsnorm(x, g):
    return x * lax.rsqrt(jnp.mean(x * x, axis=-1, keepdims=True) + EPS) * g


def _proj_kernel(x_ref, g_ref, wbig_ref, wsmall_ref, rc_ref, rs1_ref, rs2_ref,
                 qa_ref, ga_ref, qi_ref, qb_ref, kb_ref, vb_ref, gb_ref, small_ref,
                 *, idx_scale, q_scale):
    hb = _rmsnorm(x_ref[...], g_ref[...]).astype(BF16)
    rc, rs1, rs2 = rc_ref[...], rs1_ref[...], rs2_ref[...]

    def rope(y, c, s1, s2):
        return y * c + pltpu.roll(y, LANES - ROPE_DIM // 2, 1) * s1 + pltpu.roll(y, ROPE_DIM // 2, 1) * s2

    def proj(j):
        return jnp.dot(hb, wbig_ref[:, j * WIDTH:(j + 1) * WIDTH], preferred_element_type=F32)

    y = proj(0)
    for c in range(WIDTH // LANES):
        sl = slice(c * LANES, (c + 1) * LANES)
        qa_ref[:, sl] = (rope(y[:, sl], rc, rs1, rs2) * q_scale).astype(BF16)
    ga_ref[...] = proj(1).astype(BF16)
    y = proj(2)
    for c in range(WIDTH // LANES):
        sl = slice(c * LANES, (c + 1) * LANES)
        qi_ref[:, sl] = rope(y[:, sl], rc, rs1, rs2).astype(BF16)
    qb_ref[...] = (proj(3) * q_scale).astype(BF16)
    kb_ref[...] = proj(4).astype(BF16)
    vb_ref[...] = proj(5).astype(BF16)
    gb_ref[...] = proj(6).astype(BF16)

    ys = jnp.dot(hb, wsmall_ref[...], preferred_element_type=F32)
    lane = lax.broadcasted_iota(jnp.int32, rc.shape, 1)
    lo = lane < HEAD_DIM
    c_lo = jnp.where(lo, rc, 1.0)
    s1_lo = jnp.where(lo, rs1, 0.0)
    s2_lo = jnp.where(lo, rs2, 0.0)
    small_ref[:, 0:LANES] = rope(ys[:, 0:LANES], c_lo, s1_lo, s2_lo)
    wi_scale = jnp.where(lane < WI_LANE, 1.0, jnp.where(lane < FB_LANE, idx_scale, 1.0))
    small_ref[:, LANES:SMALL_W] = rope(ys[:, LANES:SMALL_W], c_lo, s1_lo, s2_lo) * wi_scale


def _split_heads(src_ref, qh_ref, TQ):
    lane = lax.broadcasted_iota(jnp.int32, (TQ, LANES), 1)
    for h in range(N_HEADS):
        q2 = src_ref[0, :, (h // 2) * LANES:(h // 2 + 1) * LANES]
        own = (lane < HEAD_DIM) if h % 2 == 0 else (lane >= HEAD_DIM)
        qh_ref[h] = jnp.where(own, q2, jnp.zeros_like(q2))


def _logit_chunk(kc, ms, *, k_chunk, addend, qh_ref, t_ref, causal_mask):
    new_ms = []
    for h in range(N_HEADS):
        t = lax.dot_general(k_chunk(h, kc), qh_ref[h], NT, preferred_element_type=F32) + addend(h, kc)
        if causal_mask is not None:
            t = jnp.where(causal_mask, t, NEG)
        t_ref[h, kc] = t
        new_ms.append(jnp.maximum(ms[h], jnp.max(t, axis=0, keepdims=True)))
    return tuple(new_ms)


def _value_chunk(kc, carry, *, vt_chunk, shift, t_ref, acc_ref):
    for h in range(N_HEADS):
        pm = jnp.exp(t_ref[h, kc] + shift[h]).astype(BF16)
        acc_ref[h] += jnp.dot(vt_chunk(h, kc), pm, preferred_element_type=F32)
    return carry


def _attend_init(acc_ref, TQ):
    acc_ref[...] = jnp.zeros(acc_ref.shape, F32)
    return tuple(jnp.full((1, TQ), -jnp.inf, F32) for _ in range(N_HEADS))


def _attend_finish(acc_ref, o_ref, TQ):
    row = lax.broadcasted_iota(jnp.int32, (LANES, TQ), 0)
    for p in range(N_HEADS // 2):
        a0, a1 = acc_ref[2 * p], acc_ref[2 * p + 1]
        o_t = jnp.where(row < HEAD_DIM, a0 / a0[HEAD_DIM:HEAD_DIM + 1, :], a1 / a1[0:1, :])
        o_ref[0, :, p * LANES:(p + 1) * LANES] = o_t.T.astype(BF16)


def _fox_kernel(q_ref, k_ref, v_ref, sm_ref, bf_ref, o_ref,
                crow_ref, cb_ref, vt_ref, qh_ref, acc_ref, t_ref, *, S, TQ):
    TK = TQ
    i = pl.program_id(1)

    @pl.when(i == 0)
    def _():
        r_io = lax.broadcasted_iota(jnp.int32, (LANES, LANES), 0)
        c_io = lax.broadcasted_iota(jnp.int32, (LANES, LANES), 1)
        tri = jnp.where(r_io >= c_io, 1.0, 0.0).astype(BF16)
        carry = jnp.zeros((1, LANES), F32)
        for r in range(S // LANES):
            rows = slice(r * LANES, (r + 1) * LANES)
            kc, off = (r * LANES) // TK, (r * LANES) % TK
            z = sm_ref[0, rows, :] + bf_ref[...]
            lf = -(jnp.maximum(-z, 0.0) + jnp.log1p(jnp.exp(-jnp.abs(z))))
            a1 = lf.astype(BF16)
            r1 = lf - a1.astype(F32)
            a2 = r1.astype(BF16)
            a3 = (r1 - a2.astype(F32)).astype(BF16)
            cb = (jnp.dot(tri, a1, preferred_element_type=F32)
                  + jnp.dot(tri, a2, preferred_element_type=F32)
                  + jnp.dot(tri, a3, preferred_element_type=F32)) + carry
            carry = cb[LANES - 1:LANES, :]
            crow_ref[kc, :, off:off + LANES] = cb.T
            for h in range(N_HEADS):
                cb_ref[h, rows, :] = jnp.broadcast_to(cb[:, FB_LANE + h:FB_LANE + h + 1], (LANES, LANES))
            for p in range(N_HEADS // 2):
                v_t = v_ref[0, rows, p * LANES:(p + 1) * LANES].astype(F32).T
                vt_ref[2 * p, kc, :, off:off + LANES] = jnp.where(r_io < HEAD_DIM, v_t, 1.0).astype(BF16)
                vt_ref[2 * p + 1, kc, :, off:off + LANES] = jnp.where(r_io < HEAD_DIM, 1.0, v_t).astype(BF16)

    _split_heads(q_ref, qh_ref, TQ)
    cq = [crow_ref[i, FB_LANE + h:FB_LANE + h + 1, :] for h in range(N_HEADS)]

    def k_chunk(h, kc):
        return k_ref[0, pl.ds(pl.multiple_of(kc * TK, TK), TK), (h // 2) * LANES:(h // 2 + 1) * LANES]

    def addend(h, kc):
        c = cb_ref[h, pl.ds(pl.multiple_of(kc * TK, TK), TK), :]
        return -jnp.concatenate([c] * (TQ // LANES), axis=1)

    step = functools.partial(_logit_chunk, k_chunk=k_chunk, addend=addend, qh_ref=qh_ref, t_ref=t_ref)
    ms = lax.fori_loop(0, i, functools.partial(step, causal_mask=None), _attend_init(acc_ref, TQ))
    diag = lax.broadcasted_iota(jnp.int32, (TK, TQ), 0) <= lax.broadcasted_iota(jnp.int32, (TK, TQ), 1)
    ms = step(i, ms, causal_mask=diag)
    shift = [cq[h] - (ms[h] + cq[h]) for h in range(N_HEADS)]
    lax.fori_loop(0, i + 1, functools.partial(_value_chunk, vt_chunk=lambda h, kc: vt_ref[h, kc],
                                              shift=shift, t_ref=t_ref, acc_ref=acc_ref), 0)
    _attend_finish(acc_ref, o_ref, TQ)


def _dsa_kernel(qi_ref, qa_ref, sm_ref, o_ref,
                ki2_ref, ka2_ref, vtlo_ref, vthi_ref, keys_ref, qh_ref, acc_ref, t_ref, *, S, TQ, K):
    TK = TQ
    i = pl.program_id(1)
    nk = i + 1

    @pl.when(i == 0)
    def _():
        lane = lax.broadcasted_iota(jnp.int32, (LANES, LANES), 1)
        lo = lane < HEAD_DIM
        for r in range(S // LANES):
            c0 = sm_ref[0, r * LANES:(r + 1) * LANES, 0:LANES]
            c1 = sm_ref[0, r * LANES:(r + 1) * LANES, LANES:SMALL_W]
            c0r = pltpu.roll(c0, HEAD_DIM, 1)
            kc, off = (r * LANES) // TK, (r * LANES) % TK
            ka2_ref[kc, off:off + LANES, :] = jnp.where(lo, c0, c0r).astype(BF16)
            ki2_ref[kc, off:off + LANES, :] = jnp.where(lo, c1, pltpu.roll(c1, HEAD_DIM, 1)).astype(BF16)
            vtlo_ref[kc, :, off:off + LANES] = jnp.where(lo, c0r, 1.0).T.astype(BF16)
            vthi_ref[kc, :, off:off + LANES] = jnp.where(lo, 1.0, c0).T.astype(BF16)

    q0 = pl.multiple_of(i * TQ, TQ)
    kpos_io = lax.broadcasted_iota(jnp.int32, (TK, TQ), 0)
    qpos = q0 + lax.broadcasted_iota(jnp.int32, (TK, TQ), 1)

    w_t = sm_ref[0, pl.ds(q0, TQ), LANES:SMALL_W].T
    _split_heads(qi_ref, qh_ref, TQ)

    def score_chunk(kc, _):
        kch = ki2_ref[kc]
        acc = jnp.zeros((TK, TQ), F32)
        for h in range(N_HEADS):
            d = lax.dot_general(kch, qh_ref[h], NT, preferred_element_type=F32)
            acc = acc + w_t[WI_LANE + h:WI_LANE + h + 1, :] * jnp.maximum(d, 0.0)
        bits = lax.bitcast_convert_type(acc, jnp.int32)
        key = bits ^ ((bits >> 31) & np.int32(0x7FFFFFFF))
        key = jnp.where(kc * TK + kpos_io <= qpos, key, INT_MIN)
        keys_ref[kc] = key.reshape(TK // 8, 8, TQ)
        return 0

    lax.fori_loop(0, nk, score_chunk, 0)

    def count(pred_fn):
        def body(kc, c8):
            return c8 + jnp.sum(jnp.where(pred_fn(keys_ref[kc]), 1.0, 0.0), axis=0)
        c8 = lax.fori_loop(0, nk, body, jnp.zeros((8, TQ), F32))
        return jnp.sum(c8, axis=0, keepdims=True)

    def bit_body(it, res):
        cand = res | lax.shift_left(jnp.int32(1), 31 - it)
        cs8 = jnp.broadcast_to(cand ^ INT_MIN, (8, TQ))[None]
        cnt = count(lambda kk: kk >= cs8)
        return jnp.where(cnt >= K, cand, res)

    res = lax.fori_loop(0, 32, bit_body, jnp.zeros((1, TQ), jnp.int32))
    thr = res ^ INT_MIN
    thr8 = jnp.broadcast_to(thr, (8, TQ))[None]
    need = K - count(lambda kk: kk > thr8)

    r_io = lax.broadcasted_iota(jnp.int32, (TK, TK), 0)
    c_io = lax.broadcasted_iota(jnp.int32, (TK, TK), 1)
    tri = jnp.where(r_io >= c_io, 1.0, 0.0).astype(BF16)

    _split_heads(qa_ref, qh_ref, TQ)

    def sel_chunk(kc, carry):
        seen, ms = carry
        kk = keys_ref[kc].reshape(TK, TQ)
        eq = kk == thr
        pre = jnp.dot(tri, jnp.where(eq, 1.0, 0.0).astype(BF16), preferred_element_type=F32) + seen
        tie = jnp.where(eq, jnp.where(pre <= need, 0.0, NEG), NEG)
        b = jnp.where(kk > thr, 0.0, tie)
        b = jnp.where(kc * TK + kpos_io <= qpos, b, NEG)
        ms = _logit_chunk(kc, ms, k_chunk=lambda h, kc: ka2_ref[kc], addend=lambda h, kc: b,
                          qh_ref=qh_ref, t_ref=t_ref, causal_mask=None)
        return pre[TK - 1:TK, :], ms

    _, ms = lax.fori_loop(0, nk, sel_chunk, (jnp.zeros((1, TQ), F32), _attend_init(acc_ref, TQ)))
    lax.fori_loop(0, nk, functools.partial(
        _value_chunk, vt_chunk=lambda h, kc: (vtlo_ref if h % 2 == 0 else vthi_ref)[kc],
        shift=[-m for m in ms], t_ref=t_ref, acc_ref=acc_ref), 0)
    _attend_finish(acc_ref, o_ref, TQ)


def _out_kernel(x_ref, aa_ref, ga_ref, ab_ref, gb_ref, p_ref, gpre_ref, gpost_ref, gple_ref,
                wm_ref, wa_ref, wb_ref, wo_ref, wp_ref, wg_ref, o_ref, *, D):
    x = x_ref[...]
    hb = _rmsnorm(x, gpre_ref[...]).astype(BF16)
    mg = jax.nn.sigmoid(jnp.dot(hb, wm_ref[...], preferred_element_type=F32))

    def branch(a_ref, g_ref, w_ref):
        g = g_ref[...].astype(F32)
        u = (a_ref[...].astype(F32) * (g * jax.nn.sigmoid(g))).astype(BF16)
        return jnp.dot(u, w_ref[...], preferred_element_type=F32)

    merged = mg[:, :D] * branch(aa_ref, ga_ref, wa_ref) + mg[:, D:] * branch(ab_ref, gb_ref, wb_ref)
    out = jnp.dot(merged.astype(BF16), wo_ref[...], preferred_element_type=F32)
    x1 = x + _rmsnorm(out, gpost_ref[...])
    e = jnp.dot(p_ref[...].astype(BF16), wp_ref[...], preferred_element_type=F32)
    gate = jax.nn.sigmoid(jnp.dot(x1.astype(BF16), wg_ref[...], preferred_element_type=F32))
    o_ref[...] = x1 + _rmsnorm(gate * e, gple_ref[...])


def _const_spec(shape):
    return pl.BlockSpec(shape, lambda *_: (0,) * len(shape), pipeline_mode=pl.Buffered(1))


def _rope_tables(S):
    half = ROPE_DIM // 2
    freqs = ROPE_THETA ** (-jnp.arange(half, dtype=F32) / half)
    ang = jnp.arange(S).astype(F32)[:, None] * freqs[None, :]
    cos, sin = jnp.cos(ang), jnp.sin(ang)
    pad = HEAD_DIM - ROPE_DIM
    c64 = jnp.concatenate([cos, cos, jnp.ones((S, pad), F32)], axis=1)
    s1 = jnp.concatenate([-sin, jnp.zeros((S, HEAD_DIM - half), F32)], axis=1)
    s2 = jnp.concatenate([jnp.zeros((S, half), F32), sin, jnp.zeros((S, pad), F32)], axis=1)
    rep = LANES // HEAD_DIM
    return jnp.tile(c64, (1, rep)), jnp.tile(s1, (1, rep)), jnp.tile(s2, (1, rep))


def _layer(x, p, w_in, b_forget, w_branch_a, w_branch_b, w_merge, w_out,
           g_pre, g_post, w_ple, w_ple_gate, g_ple):
    B, S, D = x.shape
    N = B * S
    d_ple = p.shape[-1]
    TM = 512
    TQ = 256
    assert S % TM == 0 and S % TQ == 0 and TQ % LANES == 0
    topk = min(TOPK_MAX, S // 4)
    idx_scale = (N_HEADS ** -0.5) * (HEAD_DIM ** -0.5)
    q_scale = HEAD_DIM ** -0.5

    W = WIDTH
    o_ka = W
    o_va = o_ka + HEAD_DIM
    o_ga = o_va + HEAD_DIM
    o_qi = o_ga + W
    o_ki = o_qi + W
    o_wi = o_ki + HEAD_DIM
    o_qb = o_wi + N_HEADS
    o_kb = o_qb + W
    o_vb = o_kb + W
    o_fb = o_vb + W
    o_gb = o_fb + N_HEADS
    assert o_gb + W == w_in.shape[1]
    cols = lambda o, n: w_in[:, o:o + n]
    w_big = jnp.concatenate([cols(0, W), cols(o_ga, W), cols(o_qi, W), cols(o_qb, W),
                             cols(o_kb, W), cols(o_vb, W), cols(o_gb, W)], axis=1).astype(BF16)
    w_small = jnp.concatenate([cols(o_ka, HEAD_DIM), cols(o_va, HEAD_DIM), cols(o_ki, HEAD_DIM),
                               cols(o_wi, N_HEADS), cols(o_fb, N_HEADS),
                               jnp.zeros((D, SMALL_W - 3 * HEAD_DIM - 2 * N_HEADS), w_in.dtype)],
                              axis=1).astype(BF16)
    rc, rs1, rs2 = _rope_tables(S)
    bf_pad = jnp.zeros((1, LANES), F32).at[0, FB_LANE:FB_LANE + N_HEADS].set(b_forget.astype(F32))

    params = functools.partial(pltpu.CompilerParams, vmem_limit_bytes=VMEM_LIMIT)
    x2 = x.reshape(N, D)
    tok = lambda w: pl.BlockSpec((TM, w), lambda i: (i, 0))
    rope_spec = pl.BlockSpec((TM, LANES), lambda i: (i % (S // TM), 0))
    act = jax.ShapeDtypeStruct((N, W), BF16)

    qa, ga, qi, qb, kb, vb, gb, small = pl.pallas_call(
        functools.partial(_proj_kernel, idx_scale=idx_scale, q_scale=q_scale),
        grid=(N // TM,),
        in_specs=[tok(D), _const_spec((1, D)), _const_spec((D, 7 * W)), _const_spec((D, SMALL_W)),
                  rope_spec, rope_spec, rope_spec],
        out_specs=[tok(W)] * 7 + [tok(SMALL_W)],
        out_shape=[act] * 7 + [jax.ShapeDtypeStruct((N, SMALL_W), F32)],
        compiler_params=params(dimension_semantics=("arbitrary",)),
        name="proj",
    )(x2, g_pre.reshape(1, D), w_big, w_small, rc, rs1, rs2)

    r3 = lambda a: a.reshape(B, S, a.shape[-1])
    small3 = r3(small)
    qblk = pl.BlockSpec((1, TQ, W), lambda b, i: (b, i, 0))
    seq = lambda w, j: pl.BlockSpec((1, S, w), lambda b, i: (b, 0, j))
    att = jax.ShapeDtypeStruct((B, S, W), BF16)
    head_scratch = [pltpu.VMEM((N_HEADS, TQ, LANES), BF16), pltpu.VMEM((N_HEADS, LANES, TQ), F32),
                    pltpu.VMEM((N_HEADS, S // TQ, TQ, TQ), F32)]

    att_b = pl.pallas_call(
        functools.partial(_fox_kernel, S=S, TQ=TQ),
        grid=(B, S // TQ),
        in_specs=[qblk, seq(W, 0), seq(W, 0), seq(LANES, 1),
                  pl.BlockSpec((1, LANES), lambda b, i: (0, 0))],
        out_specs=qblk,
        out_shape=att,
        scratch_shapes=[pltpu.VMEM((S // TQ, LANES, TQ), F32), pltpu.VMEM((N_HEADS, S, LANES), F32),
                        pltpu.VMEM((N_HEADS, S // TQ, LANES, TQ), BF16)] + head_scratch,
        compiler_params=params(dimension_semantics=("arbitrary", "arbitrary")),
        name="fox",
    )(r3(qb), r3(kb), r3(vb), small3, bf_pad)

    att_a = pl.pallas_call(
        functools.partial(_dsa_kernel, S=S, TQ=TQ, K=topk),
        grid=(B, S // TQ),
        in_specs=[qblk, qblk, seq(SMALL_W, 0)],
        out_specs=qblk,
        out_shape=att,
        scratch_shapes=[pltpu.VMEM((S // TQ, TQ, LANES), BF16), pltpu.VMEM((S // TQ, TQ, LANES), BF16),
                        pltpu.VMEM((S // TQ, LANES, TQ), BF16), pltpu.VMEM((S // TQ, LANES, TQ), BF16),
                        pltpu.VMEM((S // TQ, TQ // 8, 8, TQ), jnp.int32)] + head_scratch,
        compiler_params=params(dimension_semantics=("arbitrary", "arbitrary")),
        name="dsa",
    )(r3(qi), r3(qa), small3)

    bf = lambda w: w.astype(BF16)
    vec = lambda g: g.reshape(1, D)
    return pl.pallas_call(
        functools.partial(_out_kernel, D=D),
        grid=(N // TM,),
        in_specs=[tok(D), tok(W), tok(W), tok(W), tok(W), tok(d_ple),
                  _const_spec((1, D)), _const_spec((1, D)), _const_spec((1, D)),
                  _const_spec((D, 2 * D)), _const_spec((W, D)), _const_spec((W, D)),
                  _const_spec((D, D)), _const_spec((d_ple, D)), _const_spec((D, D))],
        out_specs=tok(D),
        out_shape=jax.ShapeDtypeStruct((N, D), x.dtype),
        compiler_params=params(dimension_semantics=("arbitrary",)),
        name="out",
    )(x2, att_a.reshape(N, W), ga, att_b.reshape(N, W), gb, p.reshape(N, d_ple),
      vec(g_pre), vec(g_post), vec(g_ple),
      bf(w_merge), bf(w_branch_a), bf(w_branch_b), bf(w_out), bf(w_ple), bf(w_ple_gate)).reshape(B, S, D)


def kernel(x, p, w_in, b_forget, w_branch_a, w_branch_b, w_merge, w_out, g_pre, g_post, w_ple, w_ple_gate, g_ple):
    for i in range(p.shape[0]):
        x = _layer(x, p[i], w_in[i], b_forget[i], w_branch_a[i], w_branch_b[i], w_merge[i], w_out[i],
                   g_pre[i], g_post[i], w_ple[i], w_ple_gate[i], g_ple[i])
    return x
```

```python
import functools

import numpy as np
import jax
import jax.numpy as jnp
from jax import lax
from jax.experimental import pallas as pl
from jax.experimental.pallas import tpu as pltpu

HEAD_DIM = 64
ROPE_DIM = HEAD_DIM // 4
ROPE_THETA = 500000.0
N_HEADS = 8
WIDTH = N_HEADS * HEAD_DIM
TOPK_MAX = 256
EPS = 1e-6
NEG = -1e30

LANES = 128
SMALL_W = 2 * LANES
WI_LANE = HEAD_DIM
FB_LANE = HEAD_DIM + N_HEADS
INT_MIN = np.int32(-2**31)
WORD = 32
LOG2E = 1.4426950408889634
VMEM_LIMIT = 56 * 1024 * 1024

F32 = jnp.float32
BF16 = jnp.bfloat16
NT = (((1,), (1,)), ((), ()))


def _rmsnorm(x, g):
    return x * lax.rsqrt(jnp.mean(x * x, axis=-1, keepdims=True) + EPS) * g


def _proj_kernel(x_ref, g_ref, wbig_ref, wsmall_ref, rc_ref, rs1_ref, rs2_ref,
                 qa_ref, ga_ref, qi_ref, qb_ref, kb_ref, vb_ref, gb_ref, small_ref,
                 *, idx_scale, q_scale):
    hb = _rmsnorm(x_ref[...], g_ref[...]).astype(BF16)
    rc, rs1, rs2 = rc_ref[...], rs1_ref[...], rs2_ref[...]

    def rope(y, c, s1, s2):
        return y * c + pltpu.roll(y, LANES - ROPE_DIM // 2, 1) * s1 + pltpu.roll(y, ROPE_DIM // 2, 1) * s2

    def proj(j):
        return jnp.dot(hb, wbig_ref[:, j * WIDTH:(j + 1) * WIDTH], preferred_element_type=F32)

    y = proj(0)
    for c in range(WIDTH // LANES):
        sl = slice(c * LANES, (c + 1) * LANES)
        qa_ref[:, sl] = (rope(y[:, sl], rc, rs1, rs2) * q_scale).astype(BF16)
    ga_ref[...] = proj(1).astype(BF16)
    y = proj(2)
    for c in range(WIDTH // LANES):
        sl = slice(c * LANES, (c + 1) * LANES)
        qi_ref[:, sl] = rope(y[:, sl], rc, rs1, rs2).astype(BF16)
    qb_ref[...] = (proj(3) * q_scale).astype(BF16)
    kb_ref[...] = proj(4).astype(BF16)
    vb_ref[...] = proj(5).astype(BF16)
    gb_ref[...] = proj(6).astype(BF16)

    ys = jnp.dot(hb, wsmall_ref[...], preferred_element_type=F32)
    lane = lax.broadcasted_iota(jnp.int32, rc.shape, 1)
    lo = lane < HEAD_DIM
    c_lo = jnp.where(lo, rc, 1.0)
    s1_lo = jnp.where(lo, rs1, 0.0)
    s2_lo = jnp.where(lo, rs2, 0.0)
    small_ref[:, 0:LANES] = rope(ys[:, 0:LANES], c_lo, s1_lo, s2_lo)
    wi_scale = jnp.where(lane < WI_LANE, 1.0, jnp.where(lane < FB_LANE, idx_scale, 1.0))
    small_ref[:, LANES:SMALL_W] = rope(ys[:, LANES:SMALL_W], c_lo, s1_lo, s2_lo) * wi_scale


def _split_heads(src_ref, qh_ref, TQ):
    lane = lax.broadcasted_iota(jnp.int32, (TQ, LANES), 1)
    for h in range(N_HEADS):
        q2 = src_ref[0, :, (h // 2) * LANES:(h // 2 + 1) * LANES]
        own = (lane < HEAD_DIM) if h % 2 == 0 else (lane >= HEAD_DIM)
        qh_ref[h] = jnp.where(own, q2, jnp.zeros_like(q2))


def _logit_chunk(kc, ms, *, k_chunk, addend, qh_ref, t_ref, causal_mask):
    new_ms = []
    for h in range(N_HEADS):
        t = lax.dot_general(k_chunk(h, kc), qh_ref[h], NT, preferred_element_type=F32) + addend(h, kc)
        if causal_mask is not None:
            t = jnp.where(causal_mask, t, NEG)
        t_ref[h, kc] = t
        new_ms.append(jnp.maximum(ms[h], jnp.max(t, axis=0, keepdims=True)))
    return tuple(new_ms)


def _value_chunk(kc, carry, *, vt_chunk, shift, t_ref, acc_ref):
    for h in range(N_HEADS):
        pm = jnp.exp2(t_ref[h, kc] + shift[h]).astype(BF16)
        acc_ref[h] += jnp.dot(vt_chunk(h, kc), pm, preferred_element_type=F32)
    return carry


def _chunk_loop(n, body, init):
    half = lax.shift_right_logical(n, 1)
    carry = lax.fori_loop(0, half, lambda j, c: body(2 * j + 1, body(2 * j, c)), init)
    return lax.fori_loop(2 * half, n, body, carry)


def _attend_init(acc_ref, TQ):
    acc_ref[...] = jnp.zeros(acc_ref.shape, F32)
    return tuple(jnp.full((1, TQ), -jnp.inf, F32) for _ in range(N_HEADS))


def _attend_finish(acc_ref, o_ref, TQ):
    row = lax.broadcasted_iota(jnp.int32, (LANES, TQ), 0)
    for p in range(N_HEADS // 2):
        a0, a1 = acc_ref[2 * p], acc_ref[2 * p + 1]
        o_t = jnp.where(row < HEAD_DIM, a0 / a0[HEAD_DIM:HEAD_DIM + 1, :], a1 / a1[0:1, :])
        o_ref[0, :, p * LANES:(p + 1) * LANES] = o_t.T.astype(BF16)


def _fox_kernel(q_ref, k_ref, v_ref, sm_ref, bf_ref, o_ref,
                crow_ref, cb_ref, vt_ref, qh_ref, acc_ref, t_ref, *, S, TQ):
    TK = TQ
    i = pl.program_id(1)

    @pl.when(i == 0)
    def _():
        r_io = lax.broadcasted_iota(jnp.int32, (LANES, LANES), 0)
        c_io = lax.broadcasted_iota(jnp.int32, (LANES, LANES), 1)
        tri = jnp.where(r_io >= c_io, 1.0, 0.0).astype(BF16)
        carry = jnp.zeros((1, LANES), F32)
        for r in range(S // LANES):
            rows = slice(r * LANES, (r + 1) * LANES)
            kc, off = (r * LANES) // TK, (r * LANES) % TK
            z = sm_ref[0, rows, :] + bf_ref[...]
            lf = -(jnp.maximum(-z, 0.0) + jnp.log1p(jnp.exp(-jnp.abs(z)))) * LOG2E
            a1 = lf.astype(BF16)
            r1 = lf - a1.astype(F32)
            a2 = r1.astype(BF16)
            a3 = (r1 - a2.astype(F32)).astype(BF16)
            cb = (jnp.dot(tri, a1, preferred_element_type=F32)
                  + jnp.dot(tri, a2, preferred_element_type=F32)
                  + jnp.dot(tri, a3, preferred_element_type=F32)) + carry
            carry = cb[LANES - 1:LANES, :]
            crow_ref[kc, :, off:off + LANES] = cb.T
            for h in range(N_HEADS):
                cb_ref[h, rows, :] = jnp.broadcast_to(cb[:, FB_LANE + h:FB_LANE + h + 1], (LANES, LANES))
            for p in range(N_HEADS // 2):
                v_t = v_ref[0, rows, p * LANES:(p + 1) * LANES].astype(F32).T
                vt_ref[2 * p, kc, :, off:off + LANES] = jnp.where(r_io < HEAD_DIM, v_t, 1.0).astype(BF16)
                vt_ref[2 * p + 1, kc, :, off:off + LANES] = jnp.where(r_io < HEAD_DIM, 1.0, v_t).astype(BF16)

    _split_heads(q_ref, qh_ref, TQ)
    cq = [crow_ref[i, FB_LANE + h:FB_LANE + h + 1, :] for h in range(N_HEADS)]

    def k_chunk(h, kc):
        return k_ref[0, pl.ds(pl.multiple_of(kc * TK, TK), TK), (h // 2) * LANES:(h // 2 + 1) * LANES]

    def addend(h, kc):
        c = cb_ref[h, pl.ds(pl.multiple_of(kc * TK, TK), TK), :]
        return -jnp.concatenate([c] * (TQ // LANES), axis=1)

    step = functools.partial(_logit_chunk, k_chunk=k_chunk, addend=addend, qh_ref=qh_ref, t_ref=t_ref)
    ms = _chunk_loop(i, functools.partial(step, causal_mask=None), _attend_init(acc_ref, TQ))
    diag = lax.broadcasted_iota(jnp.int32, (TK, TQ), 0) <= lax.broadcasted_iota(jnp.int32, (TK, TQ), 1)
    ms = step(i, ms, causal_mask=diag)
    shift = [cq[h] - (ms[h] + cq[h]) for h in range(N_HEADS)]
    _chunk_loop(i + 1, functools.partial(_value_chunk, vt_chunk=lambda h, kc: vt_ref[h, kc],
                                         shift=shift, t_ref=t_ref, acc_ref=acc_ref), 0)
    _attend_finish(acc_ref, o_ref, TQ)


def _bit_transpose(a):
    a = list(a)
    j, m = WORD // 2, 0x0000FFFF
    while j:
        for k in range(WORD):
            if k & j == 0:
                t = (lax.shift_right_logical(a[k], jnp.int32(j)) ^ a[k + j]) & np.int32(np.uint32(m))
                a[k + j] = a[k + j] ^ t
                a[k] = a[k] ^ lax.shift_left(t, jnp.int32(j))
        j //= 2
        m = (m ^ (m << j)) & 0xFFFFFFFF
    return a


def _dsa_kernel(qi_ref, qa_ref, sm_ref, o_ref,
                ki2_ref, ka2_ref, vtlo_ref, vthi_ref, keys_ref, planes_ref, qh_ref, acc_ref, t_ref,
                *, S, TQ, K):
    TK = TQ
    assert TK == WORD * 8
    i = pl.program_id(1)
    nk = i + 1

    @pl.when(i == 0)
    def _():
        lane = lax.broadcasted_iota(jnp.int32, (LANES, LANES), 1)
        lo = lane < HEAD_DIM
        for r in range(S // LANES):
            c0 = sm_ref[0, r * LANES:(r + 1) * LANES, 0:LANES]
            c1 = sm_ref[0, r * LANES:(r + 1) * LANES, LANES:SMALL_W]
            c0r = pltpu.roll(c0, HEAD_DIM, 1)
            kc, off = (r * LANES) // TK, (r * LANES) % TK
            ka2_ref[kc, off:off + LANES, :] = jnp.where(lo, c0, c0r).astype(BF16)
            ki2_ref[kc, off:off + LANES, :] = jnp.where(lo, c1, pltpu.roll(c1, HEAD_DIM, 1)).astype(BF16)
            vtlo_ref[kc, :, off:off + LANES] = jnp.where(lo, c0r, 1.0).T.astype(BF16)
            vthi_ref[kc, :, off:off + LANES] = jnp.where(lo, 1.0, c0).T.astype(BF16)

    q0 = pl.multiple_of(i * TQ, TQ)
    kpos_io = lax.broadcasted_iota(jnp.int32, (TK, TQ), 0)
    qpos = q0 + lax.broadcasted_iota(jnp.int32, (TK, TQ), 1)

    w_t = sm_ref[0, pl.ds(q0, TQ), LANES:SMALL_W].T
    _split_heads(qi_ref, qh_ref, TQ)

    def score_chunk(kc, _):
        kch = ki2_ref[kc]
        acc = jnp.zeros((TK, TQ), F32)
        for h in range(N_HEADS):
            d = lax.dot_general(kch, qh_ref[h], NT, preferred_element_type=F32)
            acc = acc + w_t[WI_LANE + h:WI_LANE + h + 1, :] * jnp.maximum(d, 0.0)
        bits = lax.bitcast_convert_type(acc, jnp.int32)
        key = bits ^ ((bits >> 31) & np.int32(0x7FFFFFFF))
        key = jnp.where(kc * TK + kpos_io <= qpos, key, INT_MIN)
        k3 = key.reshape(WORD, 8, TQ)
        keys_ref[kc] = k3
        planes = _bit_transpose([k3[j] for j in range(WORD)])
        planes[WORD - 1] = ~planes[WORD - 1]
        row0 = pl.multiple_of(kc * 8, 8)
        for b in range(WORD):
            planes_ref[b, pl.ds(row0, 8), :] = planes[b]
        return 0

    @pl.when(i == 0)
    def _():
        planes_ref[...] = jnp.zeros(planes_ref.shape, jnp.int32)

    _chunk_loop(nk, score_chunk, 0)

    NR = (S // TK) * 8
    r_io = lax.broadcasted_iota(jnp.int32, (NR, TQ), 0)
    q_io = lax.broadcasted_iota(jnp.int32, (NR, TQ), 1)
    n_ok = jnp.where(q_io >= (r_io & 7), ((q_io - (r_io & 7)) >> 3) + 1, 0)
    diag = jnp.where(n_ok >= WORD, -1, lax.shift_left(1, jnp.minimum(n_ok, WORD - 1)) - 1)
    live0 = jnp.where((r_io >> 3) < i, -1, jnp.where((r_io >> 3) == i, diag, 0))

    def bit_body(it, carry):
        live, above, res = carry
        b = WORD - 1 - it
        ones = live & planes_ref[b]
        tot = above + jnp.sum(lax.population_count(ones), axis=0, keepdims=True)
        take = tot >= K
        res = jnp.where(take, res | lax.shift_left(jnp.int32(1), b), res)
        above = jnp.where(take, above, tot)
        live = jnp.where(take, ones, live ^ ones)
        return live, above, res

    zero_row = jnp.zeros((1, TQ), jnp.int32)
    _, above, res = lax.fori_loop(0, WORD, bit_body, (live0, zero_row, zero_row))
    thr = res ^ INT_MIN
    need = (K - above).astype(F32)

    tri = jnp.where(lax.broadcasted_iota(jnp.int32, (TK, TK), 0) >= lax.broadcasted_iota(jnp.int32, (TK, TK), 1),
                    1.0, 0.0).astype(BF16)

    _split_heads(qa_ref, qh_ref, TQ)

    def sel_chunk(kc, carry):
        seen, ms = carry
        kk = keys_ref[kc].reshape(TK, TQ)
        eq = kk == thr
        pre = jnp.dot(tri, jnp.where(eq, 1.0, 0.0).astype(BF16), preferred_element_type=F32) + seen
        tie = jnp.where(eq, jnp.where(pre <= need, 0.0, NEG), NEG)
        b = jnp.where(kk > thr, 0.0, tie)
        b = jnp.where(kc * TK + kpos_io <= qpos, b, NEG)
        ms = _logit_chunk(kc, ms, k_chunk=lambda h, kc: ka2_ref[kc], addend=lambda h, kc: b,
                          qh_ref=qh_ref, t_ref=t_ref, causal_mask=None)
        return pre[TK - 1:TK, :], ms

    _, ms = _chunk_loop(nk, sel_chunk, (jnp.zeros((1, TQ), F32), _attend_init(acc_ref, TQ)))
    _chunk_loop(nk, functools.partial(
        _value_chunk, vt_chunk=lambda h, kc: (vtlo_ref if h % 2 == 0 else vthi_ref)[kc],
        shift=[-m for m in ms], t_ref=t_ref, acc_ref=acc_ref), 0)
    _attend_finish(acc_ref, o_ref, TQ)


def _out_kernel(x_ref, aa_ref, ga_ref, ab_ref, gb_ref, p_ref, gpre_ref, gpost_ref, gple_ref,
                wm_ref, wa_ref, wb_ref, wo_ref, wp_ref, wg_ref, o_ref, *, D):
    x = x_ref[...]
    hb = _rmsnorm(x, gpre_ref[...]).astype(BF16)
    mg = jax.nn.sigmoid(jnp.dot(hb, wm_ref[...], preferred_element_type=F32))

    def branch(a_ref, g_ref, w_ref):
        g = g_ref[...].astype(F32)
        u = (a_ref[...].astype(F32) * (g * jax.nn.sigmoid(g))).astype(BF16)
        return jnp.dot(u, w_ref[...], preferred_element_type=F32)

    merged = mg[:, :D] * branch(aa_ref, ga_ref, wa_ref) + mg[:, D:] * branch(ab_ref, gb_ref, wb_ref)
    out = jnp.dot(merged.astype(BF16), wo_ref[...], preferred_element_type=F32)
    x1 = x + _rmsnorm(out, gpost_ref[...])
    e = jnp.dot(p_ref[...].astype(BF16), wp_ref[...], preferred_element_type=F32)
    gate = jax.nn.sigmoid(jnp.dot(x1.astype(BF16), wg_ref[...], preferred_element_type=F32))
    o_ref[...] = x1 + _rmsnorm(gate * e, gple_ref[...])


def _const_spec(shape):
    return pl.BlockSpec(shape, lambda *_: (0,) * len(shape), pipeline_mode=pl.Buffered(1))


def _rope_tables(S):
    half = ROPE_DIM // 2
    freqs = ROPE_THETA ** (-jnp.arange(half, dtype=F32) / half)
    ang = jnp.arange(S).astype(F32)[:, None] * freqs[None, :]
    cos, sin = jnp.cos(ang), jnp.sin(ang)
    pad = HEAD_DIM - ROPE_DIM
    c64 = jnp.concatenate([cos, cos, jnp.ones((S, pad), F32)], axis=1)
    s1 = jnp.concatenate([-sin, jnp.zeros((S, HEAD_DIM - half), F32)], axis=1)
    s2 = jnp.concatenate([jnp.zeros((S, half), F32), sin, jnp.zeros((S, pad), F32)], axis=1)
    rep = LANES // HEAD_DIM
    return jnp.tile(c64, (1, rep)), jnp.tile(s1, (1, rep)), jnp.tile(s2, (1, rep))


def _layer(x, p, w_in, b_forget, w_branch_a, w_branch_b, w_merge, w_out,
           g_pre, g_post, w_ple, w_ple_gate, g_ple):
    B, S, D = x.shape
    N = B * S
    d_ple = p.shape[-1]
    TM = 512
    TQ = 256
    assert S % TM == 0 and S % TQ == 0 and TQ % LANES == 0
    topk = min(TOPK_MAX, S // 4)
    idx_scale = (N_HEADS ** -0.5) * (HEAD_DIM ** -0.5)
    q_scale = LOG2E * HEAD_DIM ** -0.5

    W = WIDTH
    o_ka = W
    o_va = o_ka + HEAD_DIM
    o_ga = o_va + HEAD_DIM
    o_qi = o_ga + W
    o_ki = o_qi + W
    o_wi = o_ki + HEAD_DIM
    o_qb = o_wi + N_HEADS
    o_kb = o_qb + W
    o_vb = o_kb + W
    o_fb = o_vb + W
    o_gb = o_fb + N_HEADS
    assert o_gb + W == w_in.shape[1]
    cols = lambda o, n: w_in[:, o:o + n]
    w_big = jnp.concatenate([cols(0, W), cols(o_ga, W), cols(o_qi, W), cols(o_qb, W),
                             cols(o_kb, W), cols(o_vb, W), cols(o_gb, W)], axis=1).astype(BF16)
    w_small = jnp.concatenate([cols(o_ka, HEAD_DIM), cols(o_va, HEAD_DIM), cols(o_ki, HEAD_DIM),
                               cols(o_wi, N_HEADS), cols(o_fb, N_HEADS),
                               jnp.zeros((D, SMALL_W - 3 * HEAD_DIM - 2 * N_HEADS), w_in.dtype)],
                              axis=1).astype(BF16)
    rc, rs1, rs2 = _rope_tables(S)
    bf_pad = jnp.zeros((1, LANES), F32).at[0, FB_LANE:FB_LANE + N_HEADS].set(b_forget.astype(F32))

    params = functools.partial(pltpu.CompilerParams, vmem_limit_bytes=VMEM_LIMIT)
    x2 = x.reshape(N, D)
    tok = lambda w: pl.BlockSpec((TM, w), lambda i: (i, 0))
    rope_spec = pl.BlockSpec((TM, LANES), lambda i: (i % (S // TM), 0))
    act = jax.ShapeDtypeStruct((N, W), BF16)

    qa, ga, qi, qb, kb, vb, gb, small = pl.pallas_call(
        functools.partial(_proj_kernel, idx_scale=idx_scale, q_scale=q_scale),
        grid=(N // TM,),
        in_specs=[tok(D), _const_spec((1, D)), _const_spec((D, 7 * W)), _const_spec((D, SMALL_W)),
                  rope_spec, rope_spec, rope_spec],
        out_specs=[tok(W)] * 7 + [tok(SMALL_W)],
        out_shape=[act] * 7 + [jax.ShapeDtypeStruct((N, SMALL_W), F32)],
        compiler_params=params(dimension_semantics=("arbitrary",)),
        name="proj",
    )(x2, g_pre.reshape(1, D), w_big, w_small, rc, rs1, rs2)

    r3 = lambda a: a.reshape(B, S, a.shape[-1])
    small3 = r3(small)
    qblk = pl.BlockSpec((1, TQ, W), lambda b, i: (b, i, 0))
    seq = lambda w, j: pl.BlockSpec((1, S, w), lambda b, i: (b, 0, j))
    att = jax.ShapeDtypeStruct((B, S, W), BF16)
    head_scratch = [pltpu.VMEM((N_HEADS, TQ, LANES), BF16), pltpu.VMEM((N_HEADS, LANES, TQ), F32),
                    pltpu.VMEM((N_HEADS, S // TQ, TQ, TQ), F32)]

    att_b = pl.pallas_call(
        functools.partial(_fox_kernel, S=S, TQ=TQ),
        grid=(B, S // TQ),
        in_specs=[qblk, seq(W, 0), seq(W, 0), seq(LANES, 1),
                  pl.BlockSpec((1, LANES), lambda b, i: (0, 0))],
        out_specs=qblk,
        out_shape=att,
        scratch_shapes=[pltpu.VMEM((S // TQ, LANES, TQ), F32), pltpu.VMEM((N_HEADS, S, LANES), F32),
                        pltpu.VMEM((N_HEADS, S // TQ, LANES, TQ), BF16)] + head_scratch,
        compiler_params=params(dimension_semantics=("arbitrary", "arbitrary")),
        name="fox",
    )(r3(qb), r3(kb), r3(vb), small3, bf_pad)

    att_a = pl.pallas_call(
        functools.partial(_dsa_kernel, S=S, TQ=TQ, K=topk),
        grid=(B, S // TQ),
        in_specs=[qblk, qblk, seq(SMALL_W, 0)],
        out_specs=qblk,
        out_shape=att,
        scratch_shapes=[pltpu.VMEM((S // TQ, TQ, LANES), BF16), pltpu.VMEM((S // TQ, TQ, LANES), BF16),
                        pltpu.VMEM((S // TQ, LANES, TQ), BF16), pltpu.VMEM((S // TQ, LANES, TQ), BF16),
                        pltpu.VMEM((S // TQ, TQ // 8, 8, TQ), jnp.int32),
                        pltpu.VMEM((WORD, (S // TQ) * 8, TQ), jnp.int32)] + head_scratch,
        compiler_params=params(dimension_semantics=("arbitrary", "arbitrary")),
        name="dsa",
    )(r3(qi), r3(qa), small3)

    bf = lambda w: w.astype(BF16)
    vec = lambda g: g.reshape(1, D)
    return pl.pallas_call(
        functools.partial(_out_kernel, D=D),
        grid=(N // TM,),
        in_specs=[tok(D), tok(W), tok(W), tok(W), tok(W), tok(d_ple),
                  _const_spec((1, D)), _const_spec((1, D)), _const_spec((1, D)),
                  _const_spec((D, 2 * D)), _const_spec((W, D)), _const_spec((W, D)),
                  _const_spec((D, D)), _const_spec((d_ple, D)), _const_spec((D, D))],
        out_specs=tok(D),
        out_shape=jax.ShapeDtypeStruct((N, D), x.dtype),
        compiler_params=params(dimension_semantics=("arbitrary",)),
        name="out",
    )(x2, att_a.reshape(N, W), ga, att_b.reshape(N, W), gb, p.reshape(N, d_ple),
      vec(g_pre), vec(g_post), vec(g_ple),
      bf(w_merge), bf(w_branch_a), bf(w_branch_b), bf(w_out), bf(w_ple), bf(w_ple_gate)).reshape(B, S, D)


def kernel(x, p, w_in, b_forget, w_branch_a, w_branch_b, w_merge, w_out, g_pre, g_post, w_ple, w_ple_gate, g_ple):
    for i in range(p.shape[0]):
        x = _layer(x, p[i], w_in[i], b_forget[i], w_branch_a[i], w_branch_b[i], w_merge[i], w_out[i],
                   g_pre[i], g_post[i], w_ple[i], w_ple_gate[i], g_ple[i])
    return x
```

```python
import functools

import numpy as np
import jax
import jax.numpy as jnp
from jax import lax
from jax.experimental import pallas as pl
from jax.experimental.pallas import tpu as pltpu

HEAD_DIM = 64
ROPE_DIM = HEAD_DIM // 4
ROPE_THETA = 500000.0
N_HEADS = 8
WIDTH = N_HEADS * HEAD_DIM
TOPK_MAX = 256
EPS = 1e-6
NEG = -1e30

LANES = 128
SMALL_W = 2 * LANES
WI_LANE = HEAD_DIM
FB_LANE = HEAD_DIM + N_HEADS
INT_MIN = np.int32(-2**31)
WORD = 32
LOG2E = 1.4426950408889634
VMEM_LIMIT = 56 * 1024 * 1024

F32 = jnp.float32
BF16 = jnp.bfloat16
NT = (((1,), (1,)), ((), ()))


def _rmsnorm(x, g):
    return x * lax.rsqrt(jnp.mean(x * x, axis=-1, keepdims=True) + EPS) * g


def _proj_kernel(x_ref, g_ref, wbig_ref, wsmall_ref, rc_ref, rs1_ref, rs2_ref,
                 qa_ref, ga_ref, qi_ref, qb_ref, kb_ref, vb_ref, gb_ref, small_ref,
                 *, idx_scale, q_scale):
    hb = _rmsnorm(x_ref[...], g_ref[...]).astype(BF16)
    rc, rs1, rs2 = rc_ref[...], rs1_ref[...], rs2_ref[...]

    def rope(y, c, s1, s2):
        return y * c + pltpu.roll(y, LANES - ROPE_DIM // 2, 1) * s1 + pltpu.roll(y, ROPE_DIM // 2, 1) * s2

    def proj(j):
        return jnp.dot(hb, wbig_ref[:, j * WIDTH:(j + 1) * WIDTH], preferred_element_type=F32)

    y = proj(0)
    for c in range(WIDTH // LANES):
        sl = slice(c * LANES, (c + 1) * LANES)
        qa_ref[:, sl] = (rope(y[:, sl], rc, rs1, rs2) * q_scale).astype(BF16)
    ga_ref[...] = proj(1).astype(BF16)
    y = proj(2)
    for c in range(WIDTH // LANES):
        sl = slice(c * LANES, (c + 1) * LANES)
        qi_ref[:, sl] = rope(y[:, sl], rc, rs1, rs2).astype(BF16)
    qb_ref[...] = (proj(3) * q_scale).astype(BF16)
    kb_ref[...] = proj(4).astype(BF16)
    vb_ref[...] = proj(5).astype(BF16)
    gb_ref[...] = proj(6).astype(BF16)

    ys = jnp.dot(hb, wsmall_ref[...], preferred_element_type=F32)
    lane = lax.broadcasted_iota(jnp.int32, rc.shape, 1)
    lo = lane < HEAD_DIM
    c_lo = jnp.where(lo, rc, 1.0)
    s1_lo = jnp.where(lo, rs1, 0.0)
    s2_lo = jnp.where(lo, rs2, 0.0)
    small_ref[:, 0:LANES] = rope(ys[:, 0:LANES], c_lo, s1_lo, s2_lo)
    wi_scale = jnp.where(lane < WI_LANE, 1.0, jnp.where(lane < FB_LANE, idx_scale, 1.0))
    small_ref[:, LANES:SMALL_W] = rope(ys[:, LANES:SMALL_W], c_lo, s1_lo, s2_lo) * wi_scale


def _split_heads(src_ref, qh_ref, TQ):
    lane = lax.broadcasted_iota(jnp.int32, (TQ, LANES), 1)
    for h in range(N_HEADS):
        q2 = src_ref[0, :, (h // 2) * LANES:(h // 2 + 1) * LANES]
        own = (lane < HEAD_DIM) if h % 2 == 0 else (lane >= HEAD_DIM)
        qh_ref[h] = jnp.where(own, q2, jnp.zeros_like(q2))


def _logit_chunk(kc, ms, *, k_chunk, addend, qh_ref, t_ref, causal_mask):
    new_ms = []
    for h in range(N_HEADS):
        t = lax.dot_general(k_chunk(h, kc), qh_ref[h], NT, preferred_element_type=F32) + addend(h, kc)
        if causal_mask is not None:
            t = jnp.where(causal_mask, t, NEG)
        t_ref[h, kc] = t
        new_ms.append(jnp.maximum(ms[h], jnp.max(t, axis=0, keepdims=True)))
    return tuple(new_ms)


def _value_chunk(kc, carry, *, vt_chunk, shift, t_ref, acc_ref):
    for h in range(N_HEADS):
        pm = jnp.exp2(t_ref[h, kc] + shift[h]).astype(BF16)
        acc_ref[h] += jnp.dot(vt_chunk(h, kc), pm, preferred_element_type=F32)
    return carry


def _chunk_loop(n, body, init):
    def run(lo, trips, width, carry):
        def trip(j, c):
            for u in range(width):
                c = body(lo + width * j + u, c)
            return c
        return lax.fori_loop(0, trips, trip, carry)

    quads = lax.shift_right_logical(n, 2)
    carry = run(0, quads, 4, init)
    pair = lax.shift_right_logical(n, 1) & 1
    carry = run(4 * quads, pair, 2, carry)
    return run(4 * quads + 2 * pair, n & 1, 1, carry)


def _attend_init(acc_ref, TQ):
    acc_ref[...] = jnp.zeros(acc_ref.shape, F32)
    return tuple(jnp.full((1, TQ), -jnp.inf, F32) for _ in range(N_HEADS))


def _attend_finish(acc_ref, o_ref, TQ):
    row = lax.broadcasted_iota(jnp.int32, (LANES, TQ), 0)
    for p in range(N_HEADS // 2):
        a0, a1 = acc_ref[2 * p], acc_ref[2 * p + 1]
        o_t = jnp.where(row < HEAD_DIM, a0 / a0[HEAD_DIM:HEAD_DIM + 1, :], a1 / a1[0:1, :])
        o_ref[0, :, p * LANES:(p + 1) * LANES] = o_t.T.astype(BF16)


def _fox_kernel(q_ref, k_ref, v_ref, sm_ref, bf_ref, o_ref,
                crow_ref, cb_ref, vt_ref, qh_ref, acc_ref, t_ref, *, S, TQ):
    TK = TQ
    i = pl.program_id(1)

    @pl.when(i == 0)
    def _():
        r_io = lax.broadcasted_iota(jnp.int32, (LANES, LANES), 0)
        c_io = lax.broadcasted_iota(jnp.int32, (LANES, LANES), 1)
        tri = jnp.where(r_io >= c_io, 1.0, 0.0).astype(BF16)
        carry = jnp.zeros((1, LANES), F32)
        for r in range(S // LANES):
            rows = slice(r * LANES, (r + 1) * LANES)
            kc, off = (r * LANES) // TK, (r * LANES) % TK
            z = sm_ref[0, rows, :] + bf_ref[...]
            lf = -(jnp.maximum(-z, 0.0) + jnp.log1p(jnp.exp(-jnp.abs(z)))) * LOG2E
            a1 = lf.astype(BF16)
            r1 = lf - a1.astype(F32)
            a2 = r1.astype(BF16)
            a3 = (r1 - a2.astype(F32)).astype(BF16)
            cb = (jnp.dot(tri, a1, preferred_element_type=F32)
                  + jnp.dot(tri, a2, preferred_element_type=F32)
                  + jnp.dot(tri, a3, preferred_element_type=F32)) + carry
            carry = cb[LANES - 1:LANES, :]
            crow_ref[kc, :, off:off + LANES] = cb.T
            for h in range(N_HEADS):
                cb_ref[h, rows, :] = jnp.broadcast_to(cb[:, FB_LANE + h:FB_LANE + h + 1], (LANES, LANES))
            for p in range(N_HEADS // 2):
                v_t = v_ref[0, rows, p * LANES:(p + 1) * LANES].astype(F32).T
                vt_ref[2 * p, kc, :, off:off + LANES] = jnp.where(r_io < HEAD_DIM, v_t, 1.0).astype(BF16)
                vt_ref[2 * p + 1, kc, :, off:off + LANES] = jnp.where(r_io < HEAD_DIM, 1.0, v_t).astype(BF16)

    _split_heads(q_ref, qh_ref, TQ)
    cq = [crow_ref[i, FB_LANE + h:FB_LANE + h + 1, :] for h in range(N_HEADS)]

    def k_chunk(h, kc):
        return k_ref[0, pl.ds(pl.multiple_of(kc * TK, TK), TK), (h // 2) * LANES:(h // 2 + 1) * LANES]

    def addend(h, kc):
        c = cb_ref[h, pl.ds(pl.multiple_of(kc * TK, TK), TK), :]
        return -jnp.concatenate([c] * (TQ // LANES), axis=1)

    step = functools.partial(_logit_chunk, k_chunk=k_chunk, addend=addend, qh_ref=qh_ref, t_ref=t_ref)
    ms = _chunk_loop(i, functools.partial(step, causal_mask=None), _attend_init(acc_ref, TQ))
    diag = lax.broadcasted_iota(jnp.int32, (TK, TQ), 0) <= lax.broadcasted_iota(jnp.int32, (TK, TQ), 1)
    ms = step(i, ms, causal_mask=diag)
    shift = [cq[h] - (ms[h] + cq[h]) for h in range(N_HEADS)]
    _chunk_loop(i + 1, functools.partial(_value_chunk, vt_chunk=lambda h, kc: vt_ref[h, kc],
                                         shift=shift, t_ref=t_ref, acc_ref=acc_ref), 0)
    _attend_finish(acc_ref, o_ref, TQ)


def _bit_transpose(a):
    a = list(a)
    j, m = WORD // 2, 0x0000FFFF
    while j:
        for k in range(WORD):
            if k & j == 0:
                t = (lax.shift_right_logical(a[k], jnp.int32(j)) ^ a[k + j]) & np.int32(np.uint32(m))
                a[k + j] = a[k + j] ^ t
                a[k] = a[k] ^ lax.shift_left(t, jnp.int32(j))
        j //= 2
        m = (m ^ (m << j)) & 0xFFFFFFFF
    return a


def _dsa_kernel(qi_ref, qa_ref, sm_ref, o_ref,
                ki2_ref, ka2_ref, vtlo_ref, vthi_ref, keys_ref, planes_ref, qh_ref, acc_ref, t_ref,
                *, S, TQ, K):
    TK = TQ
    assert TK == WORD * 8
    i = pl.program_id(1)
    nk = i + 1

    @pl.when(i == 0)
    def _():
        lane = lax.broadcasted_iota(jnp.int32, (LANES, LANES), 1)
        lo = lane < HEAD_DIM
        for r in range(S // LANES):
            c0 = sm_ref[0, r * LANES:(r + 1) * LANES, 0:LANES]
            c1 = sm_ref[0, r * LANES:(r + 1) * LANES, LANES:SMALL_W]
            c0r = pltpu.roll(c0, HEAD_DIM, 1)
            kc, off = (r * LANES) // TK, (r * LANES) % TK
            ka2_ref[kc, off:off + LANES, :] = jnp.where(lo, c0, c0r).astype(BF16)
            ki2_ref[kc, off:off + LANES, :] = jnp.where(lo, c1, pltpu.roll(c1, HEAD_DIM, 1)).astype(BF16)
            vtlo_ref[kc, :, off:off + LANES] = jnp.where(lo, c0r, 1.0).T.astype(BF16)
            vthi_ref[kc, :, off:off + LANES] = jnp.where(lo, 1.0, c0).T.astype(BF16)

    q0 = pl.multiple_of(i * TQ, TQ)
    kpos_io = lax.broadcasted_iota(jnp.int32, (TK, TQ), 0)
    qpos = q0 + lax.broadcasted_iota(jnp.int32, (TK, TQ), 1)

    w_t = sm_ref[0, pl.ds(q0, TQ), LANES:SMALL_W].T
    _split_heads(qi_ref, qh_ref, TQ)

    def score_chunk(kc, _):
        kch = ki2_ref[kc]
        acc = jnp.zeros((TK, TQ), F32)
        for h in range(N_HEADS):
            d = lax.dot_general(kch, qh_ref[h], NT, preferred_element_type=F32)
            acc = acc + w_t[WI_LANE + h:WI_LANE + h + 1, :] * jnp.maximum(d, 0.0)
        bits = lax.bitcast_convert_type(acc, jnp.int32)
        key = bits ^ ((bits >> 31) & np.int32(0x7FFFFFFF))
        key = jnp.where(kc * TK + kpos_io <= qpos, key, INT_MIN)
        k3 = key.reshape(WORD, 8, TQ)
        keys_ref[kc] = k3
        planes = _bit_transpose([k3[j] for j in range(WORD)])
        planes[WORD - 1] = ~planes[WORD - 1]
        row0 = pl.multiple_of(kc * 8, 8)
        for b in range(WORD):
            planes_ref[b, pl.ds(row0, 8), :] = planes[b]
        return 0

    @pl.when(i == 0)
    def _():
        planes_ref[...] = jnp.zeros(planes_ref.shape, jnp.int32)

    _chunk_loop(nk, score_chunk, 0)

    NR = (S // TK) * 8
    r_io = lax.broadcasted_iota(jnp.int32, (NR, TQ), 0)
    q_io = lax.broadcasted_iota(jnp.int32, (NR, TQ), 1)
    n_ok = jnp.where(q_io >= (r_io & 7), ((q_io - (r_io & 7)) >> 3) + 1, 0)
    diag = jnp.where(n_ok >= WORD, -1, lax.shift_left(1, jnp.minimum(n_ok, WORD - 1)) - 1)
    live0 = jnp.where((r_io >> 3) < i, -1, jnp.where((r_io >> 3) == i, diag, 0))

    def bit_body(it, carry):
        live, above, res = carry
        b = WORD - 1 - it
        ones = live & planes_ref[b]
        tot = above + jnp.sum(lax.population_count(ones), axis=0, keepdims=True)
        take = tot >= K
        res = jnp.where(take, res | lax.shift_left(jnp.int32(1), b), res)
        above = jnp.where(take, above, tot)
        live = jnp.where(take, ones, live ^ ones)
        return live, above, res

    zero_row = jnp.zeros((1, TQ), jnp.int32)
    _, above, res = lax.fori_loop(0, WORD, bit_body, (live0, zero_row, zero_row))
    thr = res ^ INT_MIN
    need = (K - above).astype(F32)

    tri = jnp.where(lax.broadcasted_iota(jnp.int32, (TK, TK), 0) >= lax.broadcasted_iota(jnp.int32, (TK, TK), 1),
                    1.0, 0.0).astype(BF16)

    _split_heads(qa_ref, qh_ref, TQ)

    def sel_chunk(kc, carry):
        seen, ms = carry
        kk = keys_ref[kc].reshape(TK, TQ)
        eq = kk == thr
        pre = jnp.dot(tri, jnp.where(eq, 1.0, 0.0).astype(BF16), preferred_element_type=F32) + seen
        tie = jnp.where(eq, jnp.where(pre <= need, 0.0, NEG), NEG)
        b = jnp.where(kk > thr, 0.0, tie)
        b = jnp.where(kc * TK + kpos_io <= qpos, b, NEG)
        ms = _logit_chunk(kc, ms, k_chunk=lambda h, kc: ka2_ref[kc], addend=lambda h, kc: b,
                          qh_ref=qh_ref, t_ref=t_ref, causal_mask=None)
        return pre[TK - 1:TK, :], ms

    _, ms = _chunk_loop(nk, sel_chunk, (jnp.zeros((1, TQ), F32), _attend_init(acc_ref, TQ)))
    _chunk_loop(nk, functools.partial(
        _value_chunk, vt_chunk=lambda h, kc: (vtlo_ref if h % 2 == 0 else vthi_ref)[kc],
        shift=[-m for m in ms], t_ref=t_ref, acc_ref=acc_ref), 0)
    _attend_finish(acc_ref, o_ref, TQ)


def _out_kernel(x_ref, aa_ref, ga_ref, ab_ref, gb_ref, p_ref, gpre_ref, gpost_ref, gple_ref,
                wm_ref, wa_ref, wb_ref, wo_ref, wp_ref, wg_ref, o_ref, *, D):
    x = x_ref[...]
    hb = _rmsnorm(x, gpre_ref[...]).astype(BF16)
    mg = jax.nn.sigmoid(jnp.dot(hb, wm_ref[...], preferred_element_type=F32))

    def branch(a_ref, g_ref, w_ref):
        g = g_ref[...].astype(F32)
        u = (a_ref[...].astype(F32) * (g * jax.nn.sigmoid(g))).astype(BF16)
        return jnp.dot(u, w_ref[...], preferred_element_type=F32)

    merged = mg[:, :D] * branch(aa_ref, ga_ref, wa_ref) + mg[:, D:] * branch(ab_ref, gb_ref, wb_ref)
    out = jnp.dot(merged.astype(BF16), wo_ref[...], preferred_element_type=F32)
    x1 = x + _rmsnorm(out, gpost_ref[...])
    e = jnp.dot(p_ref[...].astype(BF16), wp_ref[...], preferred_element_type=F32)
    gate = jax.nn.sigmoid(jnp.dot(x1.astype(BF16), wg_ref[...], preferred_element_type=F32))
    o_ref[...] = x1 + _rmsnorm(gate * e, gple_ref[...])


def _const_spec(shape):
    return pl.BlockSpec(shape, lambda *_: (0,) * len(shape), pipeline_mode=pl.Buffered(1))


def _rope_tables(S):
    half = ROPE_DIM // 2
    freqs = ROPE_THETA ** (-jnp.arange(half, dtype=F32) / half)
    ang = jnp.arange(S).astype(F32)[:, None] * freqs[None, :]
    cos, sin = jnp.cos(ang), jnp.sin(ang)
    pad = HEAD_DIM - ROPE_DIM
    c64 = jnp.concatenate([cos, cos, jnp.ones((S, pad), F32)], axis=1)
    s1 = jnp.concatenate([-sin, jnp.zeros((S, HEAD_DIM - half), F32)], axis=1)
    s2 = jnp.concatenate([jnp.zeros((S, half), F32), sin, jnp.zeros((S, pad), F32)], axis=1)
    rep = LANES // HEAD_DIM
    return jnp.tile(c64, (1, rep)), jnp.tile(s1, (1, rep)), jnp.tile(s2, (1, rep))


def _layer(x, p, w_in, b_forget, w_branch_a, w_branch_b, w_merge, w_out,
           g_pre, g_post, w_ple, w_ple_gate, g_ple):
    B, S, D = x.shape
    N = B * S
    d_ple = p.shape[-1]
    TM = 1024
    TQ = 256
    assert S % TM == 0 and S % TQ == 0 and TQ % LANES == 0
    topk = min(TOPK_MAX, S // 4)
    idx_scale = (N_HEADS ** -0.5) * (HEAD_DIM ** -0.5)
    q_scale = LOG2E * HEAD_DIM ** -0.5

    W = WIDTH
    o_ka = W
    o_va = o_ka + HEAD_DIM
    o_ga = o_va + HEAD_DIM
    o_qi = o_ga + W
    o_ki = o_qi + W
    o_wi = o_ki + HEAD_DIM
    o_qb = o_wi + N_HEADS
    o_kb = o_qb + W
    o_vb = o_kb + W
    o_fb = o_vb + W
    o_gb = o_fb + N_HEADS
    assert o_gb + W == w_in.shape[1]
    cols = lambda o, n: w_in[:, o:o + n]
    w_big = jnp.concatenate([cols(0, W), cols(o_ga, W), cols(o_qi, W), cols(o_qb, W),
                             cols(o_kb, W), cols(o_vb, W), cols(o_gb, W)], axis=1).astype(BF16)
    w_small = jnp.concatenate([cols(o_ka, HEAD_DIM), cols(o_va, HEAD_DIM), cols(o_ki, HEAD_DIM),
                               cols(o_wi, N_HEADS), cols(o_fb, N_HEADS),
                               jnp.zeros((D, SMALL_W - 3 * HEAD_DIM - 2 * N_HEADS), w_in.dtype)],
                              axis=1).astype(BF16)
    rc, rs1, rs2 = _rope_tables(S)
    bf_pad = jnp.zeros((1, LANES), F32).at[0, FB_LANE:FB_LANE + N_HEADS].set(b_forget.astype(F32))

    params = functools.partial(pltpu.CompilerParams, vmem_limit_bytes=VMEM_LIMIT)
    x2 = x.reshape(N, D)
    tok = lambda w: pl.BlockSpec((TM, w), lambda i: (i, 0))
    rope_spec = pl.BlockSpec((TM, LANES), lambda i: (i % (S // TM), 0))
    act = jax.ShapeDtypeStruct((N, W), BF16)

    qa, ga, qi, qb, kb, vb, gb, small = pl.pallas_call(
        functools.partial(_proj_kernel, idx_scale=idx_scale, q_scale=q_scale),
        grid=(N // TM,),
        in_specs=[tok(D), _const_spec((1, D)), _const_spec((D, 7 * W)), _const_spec((D, SMALL_W)),
                  rope_spec, rope_spec, rope_spec],
        out_specs=[tok(W)] * 7 + [tok(SMALL_W)],
        out_shape=[act] * 7 + [jax.ShapeDtypeStruct((N, SMALL_W), F32)],
        compiler_params=params(dimension_semantics=("arbitrary",)),
        name="proj",
    )(x2, g_pre.reshape(1, D), w_big, w_small, rc, rs1, rs2)

    r3 = lambda a: a.reshape(B, S, a.shape[-1])
    small3 = r3(small)
    qblk = pl.BlockSpec((1, TQ, W), lambda b, i: (b, i, 0))
    seq = lambda w, j: pl.BlockSpec((1, S, w), lambda b, i: (b, 0, j))
    att = jax.ShapeDtypeStruct((B, S, W), BF16)
    head_scratch = [pltpu.VMEM((N_HEADS, TQ, LANES), BF16), pltpu.VMEM((N_HEADS, LANES, TQ), F32),
                    pltpu.VMEM((N_HEADS, S // TQ, TQ, TQ), F32)]

    att_b = pl.pallas_call(
        functools.partial(_fox_kernel, S=S, TQ=TQ),
        grid=(B, S // TQ),
        in_specs=[qblk, seq(W, 0), seq(W, 0), seq(LANES, 1),
                  pl.BlockSpec((1, LANES), lambda b, i: (0, 0))],
        out_specs=qblk,
        out_shape=att,
        scratch_shapes=[pltpu.VMEM((S // TQ, LANES, TQ), F32), pltpu.VMEM((N_HEADS, S, LANES), F32),
                        pltpu.VMEM((N_HEADS, S // TQ, LANES, TQ), BF16)] + head_scratch,
        compiler_params=params(dimension_semantics=("arbitrary", "arbitrary")),
        name="fox",
    )(r3(qb), r3(kb), r3(vb), small3, bf_pad)

    att_a = pl.pallas_call(
        functools.partial(_dsa_kernel, S=S, TQ=TQ, K=topk),
        grid=(B, S // TQ),
        in_specs=[qblk, qblk, seq(SMALL_W, 0)],
        out_specs=qblk,
        out_shape=att,
        scratch_shapes=[pltpu.VMEM((S // TQ, TQ, LANES), BF16), pltpu.VMEM((S // TQ, TQ, LANES), BF16),
                        pltpu.VMEM((S // TQ, LANES, TQ), BF16), pltpu.VMEM((S // TQ, LANES, TQ), BF16),
                        pltpu.VMEM((S // TQ, TQ // 8, 8, TQ), jnp.int32),
                        pltpu.VMEM((WORD, (S // TQ) * 8, TQ), jnp.int32)] + head_scratch,
        compiler_params=params(dimension_semantics=("arbitrary", "arbitrary")),
        name="dsa",
    )(r3(qi), r3(qa), small3)

    bf = lambda w: w.astype(BF16)
    vec = lambda g: g.reshape(1, D)
    return pl.pallas_call(
        functools.partial(_out_kernel, D=D),
        grid=(N // TM,),
        in_specs=[tok(D), tok(W), tok(W), tok(W), tok(W), tok(d_ple),
                  _const_spec((1, D)), _const_spec((1, D)), _const_spec((1, D)),
                  _const_spec((D, 2 * D)), _const_spec((W, D)), _const_spec((W, D)),
                  _const_spec((D, D)), _const_spec((d_ple, D)), _const_spec((D, D))],
        out_specs=tok(D),
        out_shape=jax.ShapeDtypeStruct((N, D), x.dtype),
        compiler_params=params(dimension_semantics=("arbitrary",)),
        name="out",
    )(x2, att_a.reshape(N, W), ga, att_b.reshape(N, W), gb, p.reshape(N, d_ple),
      vec(g_pre), vec(g_post), vec(g_ple),
      bf(w_merge), bf(w_branch_a), bf(w_branch_b), bf(w_out), bf(w_ple), bf(w_ple_gate)).reshape(B, S, D)


def kernel(x, p, w_in, b_forget, w_branch_a, w_branch_b, w_merge, w_out, g_pre, g_post, w_ple, w_ple_gate, g_ple):
    for i in range(p.shape[0]):
        x = _layer(x, p[i], w_in[i], b_forget[i], w_branch_a[i], w_branch_b[i], w_merge[i], w_out[i],
                   g_pre[i], g_post[i], w_ple[i], w_ple_gate[i], g_ple[i])
    return x
```

```python
import functools

import numpy as np
import jax
import jax.numpy as jnp
from jax import lax
from jax.experimental import pallas as pl
from jax.experimental.pallas import tpu as pltpu

HEAD_DIM = 64
ROPE_DIM = HEAD_DIM // 4
ROPE_THETA = 500000.0
N_HEADS = 8
WIDTH = N_HEADS * HEAD_DIM
TOPK_MAX = 256
EPS = 1e-6
NEG = -1e30

LANES = 128
SMALL_W = 2 * LANES
WI_LANE = HEAD_DIM
FB_LANE = HEAD_DIM + N_HEADS
INT_MIN = np.int32(-2**31)
WORD = 32
LOG2E = 1.4426950408889634
VMEM_LIMIT = 56 * 1024 * 1024

F32 = jnp.float32
BF16 = jnp.bfloat16


def _rmsnorm(x, g):
    return x * lax.rsqrt(jnp.mean(x * x, axis=-1, keepdims=True) + EPS) * g


def _proj_kernel(x_ref, g_ref, wbig_ref, wsmall_ref, rc_ref, rs1_ref, rs2_ref,
                 qa_ref, ga_ref, qi_ref, qb_ref, kb_ref, vb_ref, gb_ref, small_ref,
                 *, idx_scale, q_scale):
    hb = _rmsnorm(x_ref[...], g_ref[...]).astype(BF16)
    rc, rs1, rs2 = rc_ref[...], rs1_ref[...], rs2_ref[...]

    def rope(y, c, s1, s2):
        return y * c + pltpu.roll(y, LANES - ROPE_DIM // 2, 1) * s1 + pltpu.roll(y, ROPE_DIM // 2, 1) * s2

    def proj(j):
        return jnp.dot(hb, wbig_ref[:, j * WIDTH:(j + 1) * WIDTH], preferred_element_type=F32)

    y = proj(0)
    for c in range(WIDTH // LANES):
        sl = slice(c * LANES, (c + 1) * LANES)
        qa_ref[:, sl] = (rope(y[:, sl], rc, rs1, rs2) * q_scale).astype(BF16)
    ga_ref[...] = proj(1).astype(BF16)
    y = proj(2)
    for c in range(WIDTH // LANES):
        sl = slice(c * LANES, (c + 1) * LANES)
        qi_ref[:, sl] = rope(y[:, sl], rc, rs1, rs2).astype(BF16)
    qb_ref[...] = (proj(3) * q_scale).astype(BF16)
    kb_ref[...] = proj(4).astype(BF16)
    vb_ref[...] = proj(5).astype(BF16)
    gb_ref[...] = proj(6).astype(BF16)

    ys = jnp.dot(hb, wsmall_ref[...], preferred_element_type=F32)
    lane = lax.broadcasted_iota(jnp.int32, rc.shape, 1)
    lo = lane < HEAD_DIM
    c_lo = jnp.where(lo, rc, 1.0)
    s1_lo = jnp.where(lo, rs1, 0.0)
    s2_lo = jnp.where(lo, rs2, 0.0)
    small_ref[:, 0:LANES] = rope(ys[:, 0:LANES], c_lo, s1_lo, s2_lo)
    wi_scale = jnp.where(lane < WI_LANE, 1.0, jnp.where(lane < FB_LANE, idx_scale, 1.0))
    small_ref[:, LANES:SMALL_W] = rope(ys[:, LANES:SMALL_W], c_lo, s1_lo, s2_lo) * wi_scale


def _split_heads(src_ref, qh_ref, TQ):
    row = lax.broadcasted_iota(jnp.int32, (LANES, TQ), 0)
    for p in range(N_HEADS // 2):
        q_t = src_ref[0, :, p * LANES:(p + 1) * LANES].astype(F32).T
        qh_ref[2 * p] = jnp.where(row < HEAD_DIM, q_t, 0.0).astype(BF16)
        qh_ref[2 * p + 1] = jnp.where(row < HEAD_DIM, 0.0, q_t).astype(BF16)


def _logit_chunk(kc, ms, *, k_chunk, addend, qh_ref, t_ref, causal_mask):
    new_ms = []
    for h in range(N_HEADS):
        t = jnp.dot(k_chunk(h, kc), qh_ref[h], preferred_element_type=F32) + addend(h, kc)
        if causal_mask is not None:
            t = jnp.where(causal_mask, t, NEG)
        t_ref[h, kc] = t
        new_ms.append(jnp.maximum(ms[h], jnp.max(t, axis=0, keepdims=True)))
    return tuple(new_ms)


def _value_chunk(kc, carry, *, vt_chunk, shift, t_ref, acc_ref):
    for h in range(N_HEADS):
        pm = jnp.exp2((t_ref[h, kc] + shift[h]).astype(BF16))
        acc_ref[h] += jnp.dot(vt_chunk(h, kc), pm, preferred_element_type=F32)
    return carry


def _chunk_loop(n, body, init):
    def run(lo, trips, width, carry):
        def trip(j, c):
            for u in range(width):
                c = body(lo + width * j + u, c)
            return c
        return lax.fori_loop(0, trips, trip, carry)

    quads = lax.shift_right_logical(n, 2)
    carry = run(0, quads, 4, init)
    pair = lax.shift_right_logical(n, 1) & 1
    carry = run(4 * quads, pair, 2, carry)
    return run(4 * quads + 2 * pair, n & 1, 1, carry)


def _attend_init(acc_ref, TQ):
    acc_ref[...] = jnp.zeros(acc_ref.shape, F32)
    return tuple(jnp.full((1, TQ), -jnp.inf, F32) for _ in range(N_HEADS))


def _attend_finish(acc_ref, o_ref, TQ):
    row = lax.broadcasted_iota(jnp.int32, (LANES, TQ), 0)
    for p in range(N_HEADS // 2):
        a0, a1 = acc_ref[2 * p], acc_ref[2 * p + 1]
        o_t = jnp.where(row < HEAD_DIM, a0 / a0[HEAD_DIM:HEAD_DIM + 1, :], a1 / a1[0:1, :])
        o_ref[0, :, p * LANES:(p + 1) * LANES] = o_t.T.astype(BF16)


def _fox_kernel(q_ref, k_ref, v_ref, sm_ref, bf_ref, o_ref,
                crow_ref, cb_ref, vt_ref, qh_ref, acc_ref, t_ref, *, S, TQ):
    TK = TQ
    i = pl.program_id(1)

    @pl.when(i == 0)
    def _():
        r_io = lax.broadcasted_iota(jnp.int32, (LANES, LANES), 0)
        c_io = lax.broadcasted_iota(jnp.int32, (LANES, LANES), 1)
        tri = jnp.where(r_io >= c_io, 1.0, 0.0).astype(BF16)
        carry = jnp.zeros((1, LANES), F32)
        for r in range(S // LANES):
            rows = slice(r * LANES, (r + 1) * LANES)
            kc, off = (r * LANES) // TK, (r * LANES) % TK
            z = sm_ref[0, rows, :] + bf_ref[...]
            lf = -(jnp.maximum(-z, 0.0) + jnp.log1p(jnp.exp(-jnp.abs(z)))) * LOG2E
            a1 = lf.astype(BF16)
            r1 = lf - a1.astype(F32)
            a2 = r1.astype(BF16)
            a3 = (r1 - a2.astype(F32)).astype(BF16)
            cb = (jnp.dot(tri, a1, preferred_element_type=F32)
                  + jnp.dot(tri, a2, preferred_element_type=F32)
                  + jnp.dot(tri, a3, preferred_element_type=F32)) + carry
            carry = cb[LANES - 1:LANES, :]
            crow_ref[kc, :, off:off + LANES] = cb.T
            for h in range(N_HEADS):
                cb_ref[h, rows, :] = jnp.broadcast_to(cb[:, FB_LANE + h:FB_LANE + h + 1], (LANES, LANES))
            for p in range(N_HEADS // 2):
                v_t = v_ref[0, rows, p * LANES:(p + 1) * LANES].astype(F32).T
                vt_ref[2 * p, kc, :, off:off + LANES] = jnp.where(r_io < HEAD_DIM, v_t, 1.0).astype(BF16)
                vt_ref[2 * p + 1, kc, :, off:off + LANES] = jnp.where(r_io < HEAD_DIM, 1.0, v_t).astype(BF16)

    _split_heads(q_ref, qh_ref, TQ)
    cq = [crow_ref[i, FB_LANE + h:FB_LANE + h + 1, :] for h in range(N_HEADS)]

    def k_chunk(h, kc):
        return k_ref[0, pl.ds(pl.multiple_of(kc * TK, TK), TK), (h // 2) * LANES:(h // 2 + 1) * LANES]

    def addend(h, kc):
        c = cb_ref[h, pl.ds(pl.multiple_of(kc * TK, TK), TK), :]
        return -jnp.concatenate([c] * (TQ // LANES), axis=1)

    step = functools.partial(_logit_chunk, k_chunk=k_chunk, addend=addend, qh_ref=qh_ref, t_ref=t_ref)
    ms = _chunk_loop(i, functools.partial(step, causal_mask=None), _attend_init(acc_ref, TQ))
    diag = lax.broadcasted_iota(jnp.int32, (TK, TQ), 0) <= lax.broadcasted_iota(jnp.int32, (TK, TQ), 1)
    ms = step(i, ms, causal_mask=diag)
    shift = [cq[h] - (ms[h] + cq[h]) for h in range(N_HEADS)]
    _chunk_loop(i + 1, functools.partial(_value_chunk, vt_chunk=lambda h, kc: vt_ref[h, kc],
                                         shift=shift, t_ref=t_ref, acc_ref=acc_ref), 0)
    _attend_finish(acc_ref, o_ref, TQ)


def _bit_transpose(a):
    a = list(a)
    j, m = WORD // 2, 0x0000FFFF
    while j:
        for k in range(WORD):
            if k & j == 0:
                t = (lax.shift_right_logical(a[k], jnp.int32(j)) ^ a[k + j]) & np.int32(np.uint32(m))
                a[k + j] = a[k + j] ^ t
                a[k] = a[k] ^ lax.shift_left(t, jnp.int32(j))
        j //= 2
        m = (m ^ (m << j)) & 0xFFFFFFFF
    return a


def _dsa_kernel(qi_ref, qa_ref, sm_ref, o_ref,
                ki2_ref, ka2_ref, vtlo_ref, vthi_ref, scores_ref, planes_ref, qh_ref, acc_ref, t_ref,
                *, S, TQ, K):
    TK = TQ
    assert TK == WORD * 8
    i = pl.program_id(1)
    nk = i + 1

    @pl.when(i == 0)
    def _():
        lane = lax.broadcasted_iota(jnp.int32, (LANES, LANES), 1)
        lo = lane < HEAD_DIM
        for r in range(S // LANES):
            c0 = sm_ref[0, r * LANES:(r + 1) * LANES, 0:LANES]
            c1 = sm_ref[0, r * LANES:(r + 1) * LANES, LANES:SMALL_W]
            c0r = pltpu.roll(c0, HEAD_DIM, 1)
            kc, off = (r * LANES) // TK, (r * LANES) % TK
            ka2_ref[kc, off:off + LANES, :] = jnp.where(lo, c0, c0r).astype(BF16)
            ki2_ref[kc, off:off + LANES, :] = jnp.where(lo, c1, pltpu.roll(c1, HEAD_DIM, 1)).astype(BF16)
            vtlo_ref[kc, :, off:off + LANES] = jnp.where(lo, c0r, 1.0).T.astype(BF16)
            vthi_ref[kc, :, off:off + LANES] = jnp.where(lo, 1.0, c0).T.astype(BF16)

    q0 = pl.multiple_of(i * TQ, TQ)
    kpos_io = lax.broadcasted_iota(jnp.int32, (TK, TQ), 0)
    qpos = q0 + lax.broadcasted_iota(jnp.int32, (TK, TQ), 1)

    w_t = sm_ref[0, pl.ds(q0, TQ), LANES:SMALL_W].T
    _split_heads(qi_ref, qh_ref, TQ)

    def score_chunk(kc, _):
        kch = ki2_ref[kc]
        acc = jnp.zeros((TK, TQ), F32)
        for h in range(N_HEADS):
            d = jnp.dot(kch, qh_ref[h], preferred_element_type=F32)
            acc = acc + w_t[WI_LANE + h:WI_LANE + h + 1, :] * jnp.maximum(d, 0.0)
        sc = jnp.where(kc * TK + kpos_io <= qpos, acc, -jnp.inf)
        scores_ref[kc] = sc.reshape(WORD, 8, TQ)
        bits = lax.bitcast_convert_type(sc, jnp.int32)
        k3 = (bits ^ ((bits >> 31) & np.int32(0x7FFFFFFF))).reshape(WORD, 8, TQ)
        planes = _bit_transpose([k3[j] for j in range(WORD)])
        planes[WORD - 1] = ~planes[WORD - 1]
        row0 = pl.multiple_of(kc * 8, 8)
        for b in range(WORD):
            planes_ref[b, pl.ds(row0, 8), :] = planes[b]
        return 0

    @pl.when(i == 0)
    def _():
        planes_ref[...] = jnp.zeros(planes_ref.shape, jnp.int32)

    _chunk_loop(nk, score_chunk, 0)

    NR = (S // TK) * 8
    r_io = lax.broadcasted_iota(jnp.int32, (NR, TQ), 0)
    q_io = lax.broadcasted_iota(jnp.int32, (NR, TQ), 1)
    n_ok = jnp.where(q_io >= (r_io & 7), ((q_io - (r_io & 7)) >> 3) + 1, 0)
    diag = jnp.where(n_ok >= WORD, -1, lax.shift_left(1, jnp.minimum(n_ok, WORD - 1)) - 1)
    live0 = jnp.where((r_io >> 3) < i, -1, jnp.where((r_io >> 3) == i, diag, 0))

    def bit_body(it, carry):
        live, above, res = carry
        b = WORD - 1 - it
        ones = live & planes_ref[b]
        tot = above + jnp.sum(lax.population_count(ones), axis=0, keepdims=True)
        take = tot >= K
        res = jnp.where(take, res | lax.shift_left(jnp.int32(1), b), res)
        above = jnp.where(take, above, tot)
        live = jnp.where(take, ones, live ^ ones)
        return live, above, res

    zero_row = jnp.zeros((1, TQ), jnp.int32)
    _, _, res = lax.fori_loop(0, WORD, bit_body, (live0, zero_row, zero_row))
    guess = res ^ INT_MIN
    guess = lax.bitcast_convert_type(guess ^ ((guess >> 31) & np.int32(0x7FFFFFFF)), F32)
    guess = jnp.where(res == 0, -jnp.inf, guess)

    def census(x, neighbours):
        x8 = jnp.broadcast_to(x, (8, TQ))[None]

        def body(kc, c):
            sc = scores_ref[kc]
            gt, ge = sc > x8, sc >= x8
            out = [c[0] + jnp.sum(jnp.where(gt, 1.0, 0.0), axis=0), c[1] + jnp.sum(jnp.where(ge, 1.0, 0.0), axis=0)]
            if neighbours:
                out += [jnp.minimum(c[2], jnp.min(jnp.where(gt, sc, jnp.inf), axis=0)),
                        jnp.maximum(c[3], jnp.max(jnp.where(ge, -jnp.inf, sc), axis=0))]
            return tuple(out)

        init = [jnp.zeros((8, TQ), F32)] * 2
        if neighbours:
            init += [jnp.full((8, TQ), jnp.inf, F32), jnp.full((8, TQ), -jnp.inf, F32)]
        c = lax.fori_loop(0, nk, body, tuple(init))
        red = [jnp.sum(c[0], axis=0, keepdims=True), jnp.sum(c[1], axis=0, keepdims=True)]
        if neighbours:
            red += [jnp.min(c[2], axis=0, keepdims=True), jnp.max(c[3], axis=0, keepdims=True)]
        return red

    def unsettled(n_gt, n_ge):
        return jnp.sum(jnp.where(n_gt >= K, 1.0, jnp.where(n_ge < K, 1.0, 0.0)))

    def walk(carry):
        x = carry[0]
        n_gt, n_ge, above_x, below_x = census(x, True)
        x = jnp.where(n_gt >= K, above_x, jnp.where(n_ge < K, below_x, x))
        return x, n_gt, unsettled(n_gt, n_ge)

    n_gt, n_ge = census(guess, False)
    thr, n_gt, _ = lax.while_loop(lambda c: c[2] > 0.0, walk, (guess, n_gt, unsettled(n_gt, n_ge)))
    need = K - n_gt


    tri = jnp.where(lax.broadcasted_iota(jnp.int32, (TK, TK), 0) >= lax.broadcasted_iota(jnp.int32, (TK, TK), 1),
                    1.0, 0.0).astype(BF16)

    _split_heads(qa_ref, qh_ref, TQ)

    def sel_chunk(kc, carry):
        seen, ms = carry
        kk = scores_ref[kc].reshape(TK, TQ)
        eq = kk == thr
        pre = jnp.dot(tri, jnp.where(eq, 1.0, 0.0).astype(BF16), preferred_element_type=F32) + seen
        tie = jnp.where(eq, jnp.where(pre <= need, 0.0, NEG), NEG)
        b = jnp.where(kk > thr, 0.0, tie)
        b = jnp.where(kc * TK + kpos_io <= qpos, b, NEG)
        ms = _logit_chunk(kc, ms, k_chunk=lambda h, kc: ka2_ref[kc], addend=lambda h, kc: b,
                          qh_ref=qh_ref, t_ref=t_ref, causal_mask=None)
        return pre[TK - 1:TK, :], ms

    _, ms = _chunk_loop(nk, sel_chunk, (jnp.zeros((1, TQ), F32), _attend_init(acc_ref, TQ)))
    _chunk_loop(nk, functools.partial(
        _value_chunk, vt_chunk=lambda h, kc: (vtlo_ref if h % 2 == 0 else vthi_ref)[kc],
        shift=[-m for m in ms], t_ref=t_ref, acc_ref=acc_ref), 0)
    _attend_finish(acc_ref, o_ref, TQ)


def _out_kernel(x_ref, aa_ref, ga_ref, ab_ref, gb_ref, p_ref, gpre_ref, gpost_ref, gple_ref,
                wm_ref, wa_ref, wb_ref, wo_ref, wp_ref, wg_ref, o_ref, *, D):
    x = x_ref[...]
    hb = _rmsnorm(x, gpre_ref[...]).astype(BF16)
    mg = jax.nn.sigmoid(jnp.dot(hb, wm_ref[...], preferred_element_type=F32))

    def branch(a_ref, g_ref, w_ref):
        g = g_ref[...].astype(F32)
        u = (a_ref[...].astype(F32) * (g * jax.nn.sigmoid(g))).astype(BF16)
        return jnp.dot(u, w_ref[...], preferred_element_type=F32)

    merged = mg[:, :D] * branch(aa_ref, ga_ref, wa_ref) + mg[:, D:] * branch(ab_ref, gb_ref, wb_ref)
    out = jnp.dot(merged.astype(BF16), wo_ref[...], preferred_element_type=F32)
    x1 = x + _rmsnorm(out, gpost_ref[...])
    e = jnp.dot(p_ref[...].astype(BF16), wp_ref[...], preferred_element_type=F32)
    gate = jax.nn.sigmoid(jnp.dot(x1.astype(BF16), wg_ref[...], preferred_element_type=F32))
    o_ref[...] = x1 + _rmsnorm(gate * e, gple_ref[...])


def _const_spec(shape):
    return pl.BlockSpec(shape, lambda *_: (0,) * len(shape), pipeline_mode=pl.Buffered(1))


def _rope_tables(S):
    half = ROPE_DIM // 2
    freqs = ROPE_THETA ** (-jnp.arange(half, dtype=F32) / half)
    ang = jnp.arange(S).astype(F32)[:, None] * freqs[None, :]
    cos, sin = jnp.cos(ang), jnp.sin(ang)
    pad = HEAD_DIM - ROPE_DIM
    c64 = jnp.concatenate([cos, cos, jnp.ones((S, pad), F32)], axis=1)
    s1 = jnp.concatenate([-sin, jnp.zeros((S, HEAD_DIM - half), F32)], axis=1)
    s2 = jnp.concatenate([jnp.zeros((S, half), F32), sin, jnp.zeros((S, pad), F32)], axis=1)
    rep = LANES // HEAD_DIM
    return jnp.tile(c64, (1, rep)), jnp.tile(s1, (1, rep)), jnp.tile(s2, (1, rep))


def _layer(x, p, w_in, b_forget, w_branch_a, w_branch_b, w_merge, w_out,
           g_pre, g_post, w_ple, w_ple_gate, g_ple):
    B, S, D = x.shape
    N = B * S
    d_ple = p.shape[-1]
    TM = 1024
    TQ = 256
    assert S % TM == 0 and S % TQ == 0 and TQ % LANES == 0
    topk = min(TOPK_MAX, S // 4)
    idx_scale = (N_HEADS ** -0.5) * (HEAD_DIM ** -0.5)
    q_scale = LOG2E * HEAD_DIM ** -0.5

    W = WIDTH
    o_ka = W
    o_va = o_ka + HEAD_DIM
    o_ga = o_va + HEAD_DIM
    o_qi = o_ga + W
    o_ki = o_qi + W
    o_wi = o_ki + HEAD_DIM
    o_qb = o_wi + N_HEADS
    o_kb = o_qb + W
    o_vb = o_kb + W
    o_fb = o_vb + W
    o_gb = o_fb + N_HEADS
    assert o_gb + W == w_in.shape[1]
    cols = lambda o, n: w_in[:, o:o + n]
    w_big = jnp.concatenate([cols(0, W), cols(o_ga, W), cols(o_qi, W), cols(o_qb, W),
                             cols(o_kb, W), cols(o_vb, W), cols(o_gb, W)], axis=1).astype(BF16)
    w_small = jnp.concatenate([cols(o_ka, HEAD_DIM), cols(o_va, HEAD_DIM), cols(o_ki, HEAD_DIM),
                               cols(o_wi, N_HEADS), cols(o_fb, N_HEADS),
                               jnp.zeros((D, SMALL_W - 3 * HEAD_DIM - 2 * N_HEADS), w_in.dtype)],
                              axis=1).astype(BF16)
    rc, rs1, rs2 = _rope_tables(S)
    bf_pad = jnp.zeros((1, LANES), F32).at[0, FB_LANE:FB_LANE + N_HEADS].set(b_forget.astype(F32))

    params = functools.partial(pltpu.CompilerParams, vmem_limit_bytes=VMEM_LIMIT)
    x2 = x.reshape(N, D)
    tok = lambda w: pl.BlockSpec((TM, w), lambda i: (i, 0))
    rope_spec = pl.BlockSpec((TM, LANES), lambda i: (i % (S // TM), 0))
    act = jax.ShapeDtypeStruct((N, W), BF16)

    qa, ga, qi, qb, kb, vb, gb, small = pl.pallas_call(
        functools.partial(_proj_kernel, idx_scale=idx_scale, q_scale=q_scale),
        grid=(N // TM,),
        in_specs=[tok(D), _const_spec((1, D)), _const_spec((D, 7 * W)), _const_spec((D, SMALL_W)),
                  rope_spec, rope_spec, rope_spec],
        out_specs=[tok(W)] * 7 + [tok(SMALL_W)],
        out_shape=[act] * 7 + [jax.ShapeDtypeStruct((N, SMALL_W), F32)],
        compiler_params=params(dimension_semantics=("arbitrary",)),
        name="proj",
    )(x2, g_pre.reshape(1, D), w_big, w_small, rc, rs1, rs2)

    r3 = lambda a: a.reshape(B, S, a.shape[-1])
    small3 = r3(small)
    qblk = pl.BlockSpec((1, TQ, W), lambda b, i: (b, i, 0))
    seq = lambda w, j: pl.BlockSpec((1, S, w), lambda b, i: (b, 0, j))
    att = jax.ShapeDtypeStruct((B, S, W), BF16)
    head_scratch = [pltpu.VMEM((N_HEADS, LANES, TQ), BF16), pltpu.VMEM((N_HEADS, LANES, TQ), F32),
                    pltpu.VMEM((N_HEADS, S // TQ, TQ, TQ), F32)]

    att_b = pl.pallas_call(
        functools.partial(_fox_kernel, S=S, TQ=TQ),
        grid=(B, S // TQ),
        in_specs=[qblk, seq(W, 0), seq(W, 0), seq(LANES, 1),
                  pl.BlockSpec((1, LANES), lambda b, i: (0, 0))],
        out_specs=qblk,
        out_shape=att,
        scratch_shapes=[pltpu.VMEM((S // TQ, LANES, TQ), F32), pltpu.VMEM((N_HEADS, S, LANES), F32),
                        pltpu.VMEM((N_HEADS, S // TQ, LANES, TQ), BF16)] + head_scratch,
        compiler_params=params(dimension_semantics=("arbitrary", "arbitrary")),
        name="fox",
    )(r3(qb), r3(kb), r3(vb), small3, bf_pad)

    att_a = pl.pallas_call(
        functools.partial(_dsa_kernel, S=S, TQ=TQ, K=topk),
        grid=(B, S // TQ),
        in_specs=[qblk, qblk, seq(SMALL_W, 0)],
        out_specs=qblk,
        out_shape=att,
        scratch_shapes=[pltpu.VMEM((S // TQ, TQ, LANES), BF16), pltpu.VMEM((S // TQ, TQ, LANES), BF16),
                        pltpu.VMEM((S // TQ, LANES, TQ), BF16), pltpu.VMEM((S // TQ, LANES, TQ), BF16),
                        pltpu.VMEM((S // TQ, TQ // 8, 8, TQ), F32),
                        pltpu.VMEM((WORD, (S // TQ) * 8, TQ), jnp.int32)] + head_scratch,
        compiler_params=params(dimension_semantics=("arbitrary", "arbitrary")),
        name="dsa",
    )(r3(qi), r3(qa), small3)

    bf = lambda w: w.astype(BF16)
    vec = lambda g: g.reshape(1, D)
    return pl.pallas_call(
        functools.partial(_out_kernel, D=D),
        grid=(N // TM,),
        in_specs=[tok(D), tok(W), tok(W), tok(W), tok(W), tok(d_ple),
                  _const_spec((1, D)), _const_spec((1, D)), _const_spec((1, D)),
                  _const_spec((D, 2 * D)), _const_spec((W, D)), _const_spec((W, D)),
                  _const_spec((D, D)), _const_spec((d_ple, D)), _const_spec((D, D))],
        out_specs=tok(D),
        out_shape=jax.ShapeDtypeStruct((N, D), x.dtype),
        compiler_params=params(dimension_semantics=("arbitrary",)),
        name="out",
    )(x2, att_a.reshape(N, W), ga, att_b.reshape(N, W), gb, p.reshape(N, d_ple),
      vec(g_pre), vec(g_post), vec(g_ple),
      bf(w_merge), bf(w_branch_a), bf(w_branch_b), bf(w_out), bf(w_ple), bf(w_ple_gate)).reshape(B, S, D)


def kernel(x, p, w_in, b_forget, w_branch_a, w_branch_b, w_merge, w_out, g_pre, g_post, w_ple, w_ple_gate, g_ple):
    for i in range(p.shape[0]):
        x = _layer(x, p[i], w_in[i], b_forget[i], w_branch_a[i], w_branch_b[i], w_merge[i], w_out[i],
                   g_pre[i], g_post[i], w_ple[i], w_ple_gate[i], g_ple[i])
    return x
```

```python
import functools

import numpy as np
import jax
import jax.numpy as jnp
from jax import lax
from jax.experimental import pallas as pl
from jax.experimental.pallas import tpu as pltpu

HEAD_DIM = 64
ROPE_DIM = HEAD_DIM // 4
ROPE_THETA = 500000.0
N_HEADS = 8
WIDTH = N_HEADS * HEAD_DIM
TOPK_MAX = 256
EPS = 1e-6
NEG = -1e30

LANES = 128
SMALL_W = 2 * LANES
WI_LANE = HEAD_DIM
FB_LANE = HEAD_DIM + N_HEADS
INT_MIN = np.int32(-2**31)
WORD = 32
LOG2E = 1.4426950408889634
VMEM_LIMIT = 56 * 1024 * 1024

F32 = jnp.float32
BF16 = jnp.bfloat16


def _rmsnorm(x, g):
    return x * lax.rsqrt(jnp.mean(x * x, axis=-1, keepdims=True) + EPS) * g


def _proj_kernel(x_ref, g_ref, wbig_ref, wsmall_ref, rc_ref, rs1_ref, rs2_ref,
                 qa_ref, ga_ref, qi_ref, qb_ref, kb_ref, vb_ref, gb_ref, small_ref,
                 *, idx_scale, q_scale):
    hb = _rmsnorm(x_ref[...], g_ref[...]).astype(BF16)
    rc, rs1, rs2 = rc_ref[...], rs1_ref[...], rs2_ref[...]

    def rope(y, c, s1, s2):
        return y * c + pltpu.roll(y, LANES - ROPE_DIM // 2, 1) * s1 + pltpu.roll(y, ROPE_DIM // 2, 1) * s2

    def proj(j):
        return jnp.dot(hb, wbig_ref[:, j * WIDTH:(j + 1) * WIDTH], preferred_element_type=F32)

    y = proj(0)
    for c in range(WIDTH // LANES):
        sl = slice(c * LANES, (c + 1) * LANES)
        qa_ref[:, sl] = (rope(y[:, sl], rc, rs1, rs2) * q_scale).astype(BF16)
    ga_ref[...] = proj(1).astype(BF16)
    y = proj(2)
    for c in range(WIDTH // LANES):
        sl = slice(c * LANES, (c + 1) * LANES)
        qi_ref[:, sl] = rope(y[:, sl], rc, rs1, rs2).astype(BF16)
    qb_ref[...] = (proj(3) * q_scale).astype(BF16)
    kb_ref[...] = proj(4).astype(BF16)
    vb_ref[...] = proj(5).astype(BF16)
    gb_ref[...] = proj(6).astype(BF16)

    ys = jnp.dot(hb, wsmall_ref[...], preferred_element_type=F32)
    lane = lax.broadcasted_iota(jnp.int32, rc.shape, 1)
    lo = lane < HEAD_DIM
    c_lo = jnp.where(lo, rc, 1.0)
    s1_lo = jnp.where(lo, rs1, 0.0)
    s2_lo = jnp.where(lo, rs2, 0.0)
    small_ref[:, 0:LANES] = rope(ys[:, 0:LANES], c_lo, s1_lo, s2_lo)
    wi_scale = jnp.where(lane < WI_LANE, 1.0, jnp.where(lane < FB_LANE, idx_scale, 1.0))
    small_ref[:, LANES:SMALL_W] = rope(ys[:, LANES:SMALL_W], c_lo, s1_lo, s2_lo) * wi_scale


def _split_heads(src_ref, qh_ref, TQ, q0):
    row = lax.broadcasted_iota(jnp.int32, (LANES, TQ), 0)
    for p in range(N_HEADS // 2):
        q_t = src_ref[0, pl.ds(q0, TQ), p * LANES:(p + 1) * LANES].astype(F32).T
        qh_ref[2 * p] = jnp.where(row < HEAD_DIM, q_t, 0.0).astype(BF16)
        qh_ref[2 * p + 1] = jnp.where(row < HEAD_DIM, 0.0, q_t).astype(BF16)


def _logit_chunk(kc, ms, *, k_chunk, addend, qh_ref, t_ref, causal_mask):
    new_ms = []
    for h in range(N_HEADS):
        t = jnp.dot(k_chunk(h, kc), qh_ref[h], preferred_element_type=F32) + addend(h, kc)
        if causal_mask is not None:
            t = jnp.where(causal_mask, t, NEG)
        t_ref[h, kc] = t
        new_ms.append(jnp.maximum(ms[h], jnp.max(t, axis=0, keepdims=True)))
    return tuple(new_ms)


def _value_chunk(kc, carry, *, vt_chunk, shift, t_ref, acc_ref):
    for h in range(N_HEADS):
        pm = jnp.exp2((t_ref[h, kc] + shift[h]).astype(BF16))
        acc_ref[h] += jnp.dot(vt_chunk(h, kc), pm, preferred_element_type=F32)
    return carry


def _chunk_loop(n, body, init):
    def run(lo, trips, width, carry):
        def trip(j, c):
            for u in range(width):
                c = body(lo + width * j + u, c)
            return c
        return lax.fori_loop(0, trips, trip, carry)

    quads = lax.shift_right_logical(n, 2)
    carry = run(0, quads, 4, init)
    pair = lax.shift_right_logical(n, 1) & 1
    carry = run(4 * quads, pair, 2, carry)
    return run(4 * quads + 2 * pair, n & 1, 1, carry)


def _attend_init(acc_ref, TQ):
    acc_ref[...] = jnp.zeros(acc_ref.shape, F32)
    return tuple(jnp.full((1, TQ), -jnp.inf, F32) for _ in range(N_HEADS))


def _attend_finish(acc_ref, o_ref, TQ, q0):
    row = lax.broadcasted_iota(jnp.int32, (LANES, TQ), 0)
    for p in range(N_HEADS // 2):
        a0, a1 = acc_ref[2 * p], acc_ref[2 * p + 1]
        o_t = jnp.where(row < HEAD_DIM, a0 / a0[HEAD_DIM:HEAD_DIM + 1, :], a1 / a1[0:1, :])
        o_ref[0, pl.ds(q0, TQ), p * LANES:(p + 1) * LANES] = o_t.T.astype(BF16)


def _fox_kernel(q_ref, k_ref, v_ref, sm_ref, bf_ref, o_ref,
                crow_ref, cb_ref, vt_ref, qh_ref, acc_ref, t_ref, *, S, TQ):
    TK = TQ

    r_io = lax.broadcasted_iota(jnp.int32, (LANES, LANES), 0)
    c_io = lax.broadcasted_iota(jnp.int32, (LANES, LANES), 1)
    tri = jnp.where(r_io >= c_io, 1.0, 0.0).astype(BF16)
    carry = jnp.zeros((1, LANES), F32)
    for r in range(S // LANES):
        rows = slice(r * LANES, (r + 1) * LANES)
        kc, off = (r * LANES) // TK, (r * LANES) % TK
        z = sm_ref[0, rows, :] + bf_ref[...]
        lf = -(jnp.maximum(-z, 0.0) + jnp.log1p(jnp.exp(-jnp.abs(z)))) * LOG2E
        a1 = lf.astype(BF16)
        r1 = lf - a1.astype(F32)
        a2 = r1.astype(BF16)
        a3 = (r1 - a2.astype(F32)).astype(BF16)
        cb = (jnp.dot(tri, a1, preferred_element_type=F32)
              + jnp.dot(tri, a2, preferred_element_type=F32)
              + jnp.dot(tri, a3, preferred_element_type=F32)) + carry
        carry = cb[LANES - 1:LANES, :]
        crow_ref[kc, :, off:off + LANES] = cb.T
        for h in range(N_HEADS):
            cb_ref[h, rows, :] = jnp.broadcast_to(cb[:, FB_LANE + h:FB_LANE + h + 1], (LANES, LANES))
        for p in range(N_HEADS // 2):
            v_t = v_ref[0, rows, p * LANES:(p + 1) * LANES].astype(F32).T
            vt_ref[2 * p, kc, :, off:off + LANES] = jnp.where(r_io < HEAD_DIM, v_t, 1.0).astype(BF16)
            vt_ref[2 * p + 1, kc, :, off:off + LANES] = jnp.where(r_io < HEAD_DIM, 1.0, v_t).astype(BF16)

    def k_chunk(h, kc):
        return k_ref[0, pl.ds(pl.multiple_of(kc * TK, TK), TK), (h // 2) * LANES:(h // 2 + 1) * LANES]

    def addend(h, kc):
        c = cb_ref[h, pl.ds(pl.multiple_of(kc * TK, TK), TK), :]
        return -jnp.concatenate([c] * (TQ // LANES), axis=1)

    diag = lax.broadcasted_iota(jnp.int32, (TK, TQ), 0) <= lax.broadcasted_iota(jnp.int32, (TK, TQ), 1)
    step = functools.partial(_logit_chunk, k_chunk=k_chunk, addend=addend, qh_ref=qh_ref, t_ref=t_ref)

    def query_block(i, carry):
        q0 = pl.multiple_of(i * TQ, TQ)
        _split_heads(q_ref, qh_ref, TQ, q0)
        cq = [crow_ref[i, FB_LANE + h:FB_LANE + h + 1, :] for h in range(N_HEADS)]
        ms = _chunk_loop(i, functools.partial(step, causal_mask=None), _attend_init(acc_ref, TQ))
        ms = step(i, ms, causal_mask=diag)
        shift = [cq[h] - (ms[h] + cq[h]) for h in range(N_HEADS)]
        _chunk_loop(i + 1, functools.partial(_value_chunk, vt_chunk=lambda h, kc: vt_ref[h, kc],
                                             shift=shift, t_ref=t_ref, acc_ref=acc_ref), 0)
        _attend_finish(acc_ref, o_ref, TQ, q0)
        return carry

    lax.fori_loop(0, S // TQ, query_block, 0)


def _bit_transpose(a):
    a = list(a)
    j, m = WORD // 2, 0x0000FFFF
    while j:
        for k in range(WORD):
            if k & j == 0:
                t = (lax.shift_right_logical(a[k], jnp.int32(j)) ^ a[k + j]) & np.int32(np.uint32(m))
                a[k + j] = a[k + j] ^ t
                a[k] = a[k] ^ lax.shift_left(t, jnp.int32(j))
        j //= 2
        m = (m ^ (m << j)) & 0xFFFFFFFF
    return a


def _dsa_kernel(qi_ref, qa_ref, sm_ref, o_ref,
                ki2_ref, ka2_ref, vtlo_ref, vthi_ref, scores_ref, planes_ref, qh_ref, acc_ref, t_ref,
                *, S, TQ, K):
    TK = TQ
    assert TK == WORD * 8

    planes_ref[...] = jnp.zeros(planes_ref.shape, jnp.int32)
    lane = lax.broadcasted_iota(jnp.int32, (LANES, LANES), 1)
    lo = lane < HEAD_DIM
    for r in range(S // LANES):
        c0 = sm_ref[0, r * LANES:(r + 1) * LANES, 0:LANES]
        c1 = sm_ref[0, r * LANES:(r + 1) * LANES, LANES:SMALL_W]
        c0r = pltpu.roll(c0, HEAD_DIM, 1)
        kc, off = (r * LANES) // TK, (r * LANES) % TK
        ka2_ref[kc, off:off + LANES, :] = jnp.where(lo, c0, c0r).astype(BF16)
        ki2_ref[kc, off:off + LANES, :] = jnp.where(lo, c1, pltpu.roll(c1, HEAD_DIM, 1)).astype(BF16)
        vtlo_ref[kc, :, off:off + LANES] = jnp.where(lo, c0r, 1.0).T.astype(BF16)
        vthi_ref[kc, :, off:off + LANES] = jnp.where(lo, 1.0, c0).T.astype(BF16)

    kpos_io = lax.broadcasted_iota(jnp.int32, (TK, TQ), 0)
    tri = jnp.where(lax.broadcasted_iota(jnp.int32, (TK, TK), 0) >= lax.broadcasted_iota(jnp.int32, (TK, TK), 1),
                    1.0, 0.0).astype(BF16)
    NR = (S // TK) * 8
    r_io = lax.broadcasted_iota(jnp.int32, (NR, TQ), 0)
    q_io = lax.broadcasted_iota(jnp.int32, (NR, TQ), 1)
    n_ok = jnp.where(q_io >= (r_io & 7), ((q_io - (r_io & 7)) >> 3) + 1, 0)
    live_diag = jnp.where(n_ok >= WORD, -1, lax.shift_left(1, jnp.minimum(n_ok, WORD - 1)) - 1)

    def query_block(i, carry):
        nk = i + 1
        q0 = pl.multiple_of(i * TQ, TQ)
        qpos = q0 + lax.broadcasted_iota(jnp.int32, (TK, TQ), 1)

        w_t = sm_ref[0, pl.ds(q0, TQ), LANES:SMALL_W].T
        _split_heads(qi_ref, qh_ref, TQ, q0)

        def score_chunk(kc, _):
            kch = ki2_ref[kc]
            acc = jnp.zeros((TK, TQ), F32)
            for h in range(N_HEADS):
                d = jnp.dot(kch, qh_ref[h], preferred_element_type=F32)
                acc = acc + w_t[WI_LANE + h:WI_LANE + h + 1, :] * jnp.maximum(d, 0.0)
            sc = jnp.where(kc * TK + kpos_io <= qpos, acc, -jnp.inf)
            scores_ref[kc] = sc.reshape(WORD, 8, TQ)
            bits = lax.bitcast_convert_type(sc, jnp.int32)
            k3 = (bits ^ ((bits >> 31) & np.int32(0x7FFFFFFF))).reshape(WORD, 8, TQ)
            planes = _bit_transpose([k3[j] for j in range(WORD)])
            planes[WORD - 1] = ~planes[WORD - 1]
            row0 = pl.multiple_of(kc * 8, 8)
            for b in range(WORD):
                planes_ref[b, pl.ds(row0, 8), :] = planes[b]
            return 0

        _chunk_loop(nk, score_chunk, 0)

        live0 = jnp.where((r_io >> 3) < i, -1, jnp.where((r_io >> 3) == i, live_diag, 0))

        def bit_body(it, carry):
            live, above, res = carry
            b = WORD - 1 - it
            ones = live & planes_ref[b]
            tot = above + jnp.sum(lax.population_count(ones), axis=0, keepdims=True)
            take = tot >= K
            res = jnp.where(take, res | lax.shift_left(jnp.int32(1), b), res)
            above = jnp.where(take, above, tot)
            live = jnp.where(take, ones, live ^ ones)
            return live, above, res

        zero_row = jnp.zeros((1, TQ), jnp.int32)
        _, _, res = lax.fori_loop(0, WORD, bit_body, (live0, zero_row, zero_row))
        guess = res ^ INT_MIN
        guess = lax.bitcast_convert_type(guess ^ ((guess >> 31) & np.int32(0x7FFFFFFF)), F32)
        guess = jnp.where(res == 0, -jnp.inf, guess)

        def census(x, neighbours):
            x8 = jnp.broadcast_to(x, (8, TQ))[None]

            def body(kc, c):
                sc = scores_ref[kc]
                gt, ge = sc > x8, sc >= x8
                out = [c[0] + jnp.sum(jnp.where(gt, 1.0, 0.0), axis=0), c[1] + jnp.sum(jnp.where(ge, 1.0, 0.0), axis=0)]
                if neighbours:
                    out += [jnp.minimum(c[2], jnp.min(jnp.where(gt, sc, jnp.inf), axis=0)),
                            jnp.maximum(c[3], jnp.max(jnp.where(ge, -jnp.inf, sc), axis=0))]
                return tuple(out)

            init = [jnp.zeros((8, TQ), F32)] * 2
            if neighbours:
                init += [jnp.full((8, TQ), jnp.inf, F32), jnp.full((8, TQ), -jnp.inf, F32)]
            c = lax.fori_loop(0, nk, body, tuple(init))
            red = [jnp.sum(c[0], axis=0, keepdims=True), jnp.sum(c[1], axis=0, keepdims=True)]
            if neighbours:
                red += [jnp.min(c[2], axis=0, keepdims=True), jnp.max(c[3], axis=0, keepdims=True)]
            return red

        def unsettled(n_gt, n_ge):
            return jnp.sum(jnp.where(n_gt >= K, 1.0, jnp.where(n_ge < K, 1.0, 0.0)))

        def walk(carry):
            x = carry[0]
            n_gt, n_ge, above_x, below_x = census(x, True)
            x = jnp.where(n_gt >= K, above_x, jnp.where(n_ge < K, below_x, x))
            return x, n_gt, unsettled(n_gt, n_ge)

        n_gt, n_ge = census(guess, False)
        thr, n_gt, _ = lax.while_loop(lambda c: c[2] > 0.0, walk, (guess, n_gt, unsettled(n_gt, n_ge)))
        need = K - n_gt

        _split_heads(qa_ref, qh_ref, TQ, q0)

        def sel_chunk(kc, carry):
            seen, ms = carry
            kk = scores_ref[kc].reshape(TK, TQ)
            eq = kk == thr
            pre = jnp.dot(tri, jnp.where(eq, 1.0, 0.0).astype(BF16), preferred_element_type=F32) + seen
            tie = jnp.where(eq, jnp.where(pre <= need, 0.0, NEG), NEG)
            b = jnp.where(kk > thr, 0.0, tie)
            b = jnp.where(kc * TK + kpos_io <= qpos, b, NEG)
            ms = _logit_chunk(kc, ms, k_chunk=lambda h, kc: ka2_ref[kc], addend=lambda h, kc: b,
                              qh_ref=qh_ref, t_ref=t_ref, causal_mask=None)
            return pre[TK - 1:TK, :], ms

        _, ms = _chunk_loop(nk, sel_chunk, (jnp.zeros((1, TQ), F32), _attend_init(acc_ref, TQ)))
        _chunk_loop(nk, functools.partial(
            _value_chunk, vt_chunk=lambda h, kc: (vtlo_ref if h % 2 == 0 else vthi_ref)[kc],
            shift=[-m for m in ms], t_ref=t_ref, acc_ref=acc_ref), 0)
        _attend_finish(acc_ref, o_ref, TQ, q0)
        return carry

    lax.fori_loop(0, S // TQ, query_block, 0)


def _out_kernel(x_ref, aa_ref, ga_ref, ab_ref, gb_ref, p_ref, gpre_ref, gpost_ref, gple_ref,
                wm_ref, wa_ref, wb_ref, wo_ref, wp_ref, wg_ref, o_ref, *, D):
    x = x_ref[...]
    hb = _rmsnorm(x, gpre_ref[...]).astype(BF16)
    mg = jax.nn.sigmoid(jnp.dot(hb, wm_ref[...], preferred_element_type=F32))

    def branch(a_ref, g_ref, w_ref):
        g = g_ref[...].astype(F32)
        u = (a_ref[...].astype(F32) * (g * jax.nn.sigmoid(g))).astype(BF16)
        return jnp.dot(u, w_ref[...], preferred_element_type=F32)

    merged = mg[:, :D] * branch(aa_ref, ga_ref, wa_ref) + mg[:, D:] * branch(ab_ref, gb_ref, wb_ref)
    out = jnp.dot(merged.astype(BF16), wo_ref[...], preferred_element_type=F32)
    x1 = x + _rmsnorm(out, gpost_ref[...])
    e = jnp.dot(p_ref[...].astype(BF16), wp_ref[...], preferred_element_type=F32)
    gate = jax.nn.sigmoid(jnp.dot(x1.astype(BF16), wg_ref[...], preferred_element_type=F32))
    o_ref[...] = x1 + _rmsnorm(gate * e, gple_ref[...])


def _const_spec(shape):
    return pl.BlockSpec(shape, lambda *_: (0,) * len(shape), pipeline_mode=pl.Buffered(1))


def _rope_tables(S):
    half = ROPE_DIM // 2
    freqs = ROPE_THETA ** (-jnp.arange(half, dtype=F32) / half)
    ang = jnp.arange(S).astype(F32)[:, None] * freqs[None, :]
    cos, sin = jnp.cos(ang), jnp.sin(ang)
    pad = HEAD_DIM - ROPE_DIM
    c64 = jnp.concatenate([cos, cos, jnp.ones((S, pad), F32)], axis=1)
    s1 = jnp.concatenate([-sin, jnp.zeros((S, HEAD_DIM - half), F32)], axis=1)
    s2 = jnp.concatenate([jnp.zeros((S, half), F32), sin, jnp.zeros((S, pad), F32)], axis=1)
    rep = LANES // HEAD_DIM
    return jnp.tile(c64, (1, rep)), jnp.tile(s1, (1, rep)), jnp.tile(s2, (1, rep))


def _layer(x, p, w_in, b_forget, w_branch_a, w_branch_b, w_merge, w_out,
           g_pre, g_post, w_ple, w_ple_gate, g_ple):
    B, S, D = x.shape
    N = B * S
    d_ple = p.shape[-1]
    TM = 1024
    TQ = 256
    assert S % TM == 0 and S % TQ == 0 and TQ % LANES == 0
    topk = min(TOPK_MAX, S // 4)
    idx_scale = (N_HEADS ** -0.5) * (HEAD_DIM ** -0.5)
    q_scale = LOG2E * HEAD_DIM ** -0.5

    W = WIDTH
    o_ka = W
    o_va = o_ka + HEAD_DIM
    o_ga = o_va + HEAD_DIM
    o_qi = o_ga + W
    o_ki = o_qi + W
    o_wi = o_ki + HEAD_DIM
    o_qb = o_wi + N_HEADS
    o_kb = o_qb + W
    o_vb = o_kb + W
    o_fb = o_vb + W
    o_gb = o_fb + N_HEADS
    assert o_gb + W == w_in.shape[1]
    cols = lambda o, n: w_in[:, o:o + n]
    w_big = jnp.concatenate([cols(0, W), cols(o_ga, W), cols(o_qi, W), cols(o_qb, W),
                             cols(o_kb, W), cols(o_vb, W), cols(o_gb, W)], axis=1).astype(BF16)
    w_small = jnp.concatenate([cols(o_ka, HEAD_DIM), cols(o_va, HEAD_DIM), cols(o_ki, HEAD_DIM),
                               cols(o_wi, N_HEADS), cols(o_fb, N_HEADS),
                               jnp.zeros((D, SMALL_W - 3 * HEAD_DIM - 2 * N_HEADS), w_in.dtype)],
                              axis=1).astype(BF16)
    rc, rs1, rs2 = _rope_tables(S)
    bf_pad = jnp.zeros((1, LANES), F32).at[0, FB_LANE:FB_LANE + N_HEADS].set(b_forget.astype(F32))

    params = functools.partial(pltpu.CompilerParams, vmem_limit_bytes=VMEM_LIMIT,
                               dimension_semantics=("arbitrary",))
    x2 = x.reshape(N, D)
    tok = lambda w: pl.BlockSpec((TM, w), lambda i: (i, 0))
    rope_spec = pl.BlockSpec((TM, LANES), lambda i: (i % (S // TM), 0))
    act = jax.ShapeDtypeStruct((N, W), BF16)

    qa, ga, qi, qb, kb, vb, gb, small = pl.pallas_call(
        functools.partial(_proj_kernel, idx_scale=idx_scale, q_scale=q_scale),
        grid=(N // TM,),
        in_specs=[tok(D), _const_spec((1, D)), _const_spec((D, 7 * W)), _const_spec((D, SMALL_W)),
                  rope_spec, rope_spec, rope_spec],
        out_specs=[tok(W)] * 7 + [tok(SMALL_W)],
        out_shape=[act] * 7 + [jax.ShapeDtypeStruct((N, SMALL_W), F32)],
        compiler_params=params(),
        name="proj",
    )(x2, g_pre.reshape(1, D), w_big, w_small, rc, rs1, rs2)

    r3 = lambda a: a.reshape(B, S, a.shape[-1])
    small3 = r3(small)
    seq = lambda w, j: pl.BlockSpec((1, S, w), lambda b: (b, 0, j))
    att = jax.ShapeDtypeStruct((B, S, W), BF16)
    head_scratch = [pltpu.VMEM((N_HEADS, LANES, TQ), BF16), pltpu.VMEM((N_HEADS, LANES, TQ), F32),
                    pltpu.VMEM((N_HEADS, S // TQ, TQ, TQ), F32)]

    att_b = pl.pallas_call(
        functools.partial(_fox_kernel, S=S, TQ=TQ),
        grid=(B,),
        in_specs=[seq(W, 0), seq(W, 0), seq(W, 0), seq(LANES, 1),
                  pl.BlockSpec((1, LANES), lambda b: (0, 0))],
        out_specs=seq(W, 0),
        out_shape=att,
        scratch_shapes=[pltpu.VMEM((S // TQ, LANES, TQ), F32), pltpu.VMEM((N_HEADS, S, LANES), F32),
                        pltpu.VMEM((N_HEADS, S // TQ, LANES, TQ), BF16)] + head_scratch,
        compiler_params=params(),
        name="fox",
    )(r3(qb), r3(kb), r3(vb), small3, bf_pad)

    att_a = pl.pallas_call(
        functools.partial(_dsa_kernel, S=S, TQ=TQ, K=topk),
        grid=(B,),
        in_specs=[seq(W, 0), seq(W, 0), seq(SMALL_W, 0)],
        out_specs=seq(W, 0),
        out_shape=att,
        scratch_shapes=[pltpu.VMEM((S // TQ, TQ, LANES), BF16), pltpu.VMEM((S // TQ, TQ, LANES), BF16),
                        pltpu.VMEM((S // TQ, LANES, TQ), BF16), pltpu.VMEM((S // TQ, LANES, TQ), BF16),
                        pltpu.VMEM((S // TQ, TQ // 8, 8, TQ), F32),
                        pltpu.VMEM((WORD, (S // TQ) * 8, TQ), jnp.int32)] + head_scratch,
        compiler_params=params(),
        name="dsa",
    )(r3(qi), r3(qa), small3)

    bf = lambda w: w.astype(BF16)
    vec = lambda g: g.reshape(1, D)
    return pl.pallas_call(
        functools.partial(_out_kernel, D=D),
        grid=(N // TM,),
        in_specs=[tok(D), tok(W), tok(W), tok(W), tok(W), tok(d_ple),
                  _const_spec((1, D)), _const_spec((1, D)), _const_spec((1, D)),
                  _const_spec((D, 2 * D)), _const_spec((W, D)), _const_spec((W, D)),
                  _const_spec((D, D)), _const_spec((d_ple, D)), _const_spec((D, D))],
        out_specs=tok(D),
        out_shape=jax.ShapeDtypeStruct((N, D), x.dtype),
        compiler_params=params(),
        name="out",
    )(x2, att_a.reshape(N, W), ga, att_b.reshape(N, W), gb, p.reshape(N, d_ple),
      vec(g_pre), vec(g_post), vec(g_ple),
      bf(w_merge), bf(w_branch_a), bf(w_branch_b), bf(w_out), bf(w_ple), bf(w_ple_gate)).reshape(B, S, D)


def kernel(x, p, w_in, b_forget, w_branch_a, w_branch_b, w_merge, w_out, g_pre, g_post, w_ple, w_ple_gate, g_ple):
    for i in range(p.shape[0]):
        x = _layer(x, p[i], w_in[i], b_forget[i], w_branch_a[i], w_branch_b[i], w_merge[i], w_out[i],
                   g_pre[i], g_post[i], w_ple[i], w_ple_gate[i], g_ple[i])
    return x
```

```python
import functools

import numpy as np
import jax
import jax.numpy as jnp
from jax import lax
from jax.experimental import pallas as pl
from jax.experimental.pallas import tpu as pltpu

HEAD_DIM = 64
ROPE_DIM = HEAD_DIM // 4
ROPE_THETA = 500000.0
N_HEADS = 8
WIDTH = N_HEADS * HEAD_DIM
TOPK_MAX = 256
EPS = 1e-6
NEG = -1e30

LANES = 128
SMALL_W = 2 * LANES
WI_LANE = HEAD_DIM
FB_LANE = HEAD_DIM + N_HEADS
INT_MIN = np.int32(-2**31)
WORD = 32
LOG2E = 1.4426950408889634
VMEM_LIMIT = 56 * 1024 * 1024

F32 = jnp.float32
BF16 = jnp.bfloat16


def _rmsnorm(x, g):
    return x * lax.rsqrt(jnp.mean(x * x, axis=-1, keepdims=True) + EPS) * g


def _proj_kernel(x_ref, g_ref, wbig_ref, wsmall_ref, rc_ref, rs1_ref, rs2_ref,
                 qa_ref, ga_ref, qi_ref, qb_ref, kb_ref, vb_ref, gb_ref, small_ref,
                 *, idx_scale, q_scale):
    hb = _rmsnorm(x_ref[...], g_ref[...]).astype(BF16)
    rc, rs1, rs2 = rc_ref[...], rs1_ref[...], rs2_ref[...]

    def rope(y, c, s1, s2):
        return y * c + pltpu.roll(y, LANES - ROPE_DIM // 2, 1) * s1 + pltpu.roll(y, ROPE_DIM // 2, 1) * s2

    def proj(j):
        return jnp.dot(hb, wbig_ref[:, j * WIDTH:(j + 1) * WIDTH], preferred_element_type=F32)

    y = proj(0)
    for c in range(WIDTH // LANES):
        sl = slice(c * LANES, (c + 1) * LANES)
        qa_ref[:, sl] = (rope(y[:, sl], rc, rs1, rs2) * q_scale).astype(BF16)
    ga_ref[...] = proj(1).astype(BF16)
    y = proj(2)
    for c in range(WIDTH // LANES):
        sl = slice(c * LANES, (c + 1) * LANES)
        qi_ref[:, sl] = rope(y[:, sl], rc, rs1, rs2).astype(BF16)
    qb_ref[...] = (proj(3) * q_scale).astype(BF16)
    kb_ref[...] = proj(4).astype(BF16)
    vb_ref[...] = proj(5).astype(BF16)
    gb_ref[...] = proj(6).astype(BF16)

    ys = jnp.dot(hb, wsmall_ref[...], preferred_element_type=F32)
    lane = lax.broadcasted_iota(jnp.int32, rc.shape, 1)
    lo = lane < HEAD_DIM
    c_lo = jnp.where(lo, rc, 1.0)
    s1_lo = jnp.where(lo, rs1, 0.0)
    s2_lo = jnp.where(lo, rs2, 0.0)
    small_ref[:, 0:LANES] = rope(ys[:, 0:LANES], c_lo, s1_lo, s2_lo)
    wi_scale = jnp.where(lane < WI_LANE, 1.0, jnp.where(lane < FB_LANE, idx_scale, 1.0))
    small_ref[:, LANES:SMALL_W] = rope(ys[:, LANES:SMALL_W], c_lo, s1_lo, s2_lo) * wi_scale


def _split_heads(src_ref, qh_ref, TQ, q0):
    row = lax.broadcasted_iota(jnp.int32, (LANES, TQ), 0)
    for p in range(N_HEADS // 2):
        q_t = src_ref[0, pl.ds(q0, TQ), p * LANES:(p + 1) * LANES].astype(F32).T
        qh_ref[2 * p, 0:LANES, :] = jnp.where(row < HEAD_DIM, q_t, 0.0).astype(BF16)
        qh_ref[2 * p + 1, 0:LANES, :] = jnp.where(row < HEAD_DIM, 0.0, q_t).astype(BF16)


def _logit_chunk(kc, ms, *, k_chunk, addend, qh_ref, t_ref, causal_mask):
    new_ms = []
    for h in range(N_HEADS):
        t = jnp.dot(k_chunk(h, kc), qh_ref[h], preferred_element_type=F32)
        if addend is not None:
            t = t + addend(h, kc)
        if causal_mask is not None:
            t = jnp.where(causal_mask, t, NEG)
        t_ref[h, kc] = t
        new_ms.append(jnp.maximum(ms[h], jnp.max(t, axis=0, keepdims=True)))
    return tuple(new_ms)


def _value_chunk(kc, carry, *, vt_chunk, shift, t_ref, acc_ref):
    for h in range(N_HEADS):
        pm = jnp.exp2((t_ref[h, kc] + shift[h]).astype(BF16))
        acc_ref[h] += jnp.dot(vt_chunk(h, kc), pm, preferred_element_type=F32)
    return carry


def _chunk_loop(n, body, init):
    def run(lo, trips, width, carry):
        def trip(j, c):
            for u in range(width):
                c = body(lo + width * j + u, c)
            return c
        return lax.fori_loop(0, trips, trip, carry)

    quads = lax.shift_right_logical(n, 2)
    carry = run(0, quads, 4, init)
    pair = lax.shift_right_logical(n, 1) & 1
    carry = run(4 * quads, pair, 2, carry)
    return run(4 * quads + 2 * pair, n & 1, 1, carry)


def _attend_init(acc_ref, TQ):
    acc_ref[...] = jnp.zeros(acc_ref.shape, F32)
    return tuple(jnp.full((1, TQ), -jnp.inf, F32) for _ in range(N_HEADS))


def _attend_finish(acc_ref, o_ref, TQ, q0):
    row = lax.broadcasted_iota(jnp.int32, (LANES, TQ), 0)
    for p in range(N_HEADS // 2):
        a0, a1 = acc_ref[2 * p], acc_ref[2 * p + 1]
        o_t = jnp.where(row < HEAD_DIM, a0 / a0[HEAD_DIM:HEAD_DIM + 1, :], a1 / a1[0:1, :])
        o_ref[0, pl.ds(q0, TQ), p * LANES:(p + 1) * LANES] = o_t.T.astype(BF16)


def _fox_kernel(q_ref, k_ref, v_ref, sm_ref, bf_ref, o_ref,
                crow_ref, kaug_ref, vt_ref, qh_ref, acc_ref, t_ref, *, S, TQ):
    TK = TQ

    r_io = lax.broadcasted_iota(jnp.int32, (LANES, LANES), 0)
    c_io = lax.broadcasted_iota(jnp.int32, (LANES, LANES), 1)
    tri = jnp.where(r_io >= c_io, 1.0, 0.0).astype(BF16)
    carry = jnp.zeros((1, LANES), F32)
    for r in range(S // LANES):
        rows = slice(r * LANES, (r + 1) * LANES)
        kc, off = (r * LANES) // TK, (r * LANES) % TK
        z = sm_ref[0, rows, :] + bf_ref[...]
        lf = -(jnp.maximum(-z, 0.0) + jnp.log1p(jnp.exp(-jnp.abs(z)))) * LOG2E
        a1 = lf.astype(BF16)
        r1 = lf - a1.astype(F32)
        a2 = r1.astype(BF16)
        a3 = (r1 - a2.astype(F32)).astype(BF16)
        cb = (jnp.dot(tri, a1, preferred_element_type=F32)
              + jnp.dot(tri, a2, preferred_element_type=F32)
              + jnp.dot(tri, a3, preferred_element_type=F32)) + carry
        carry = cb[LANES - 1:LANES, :]
        crow_ref[kc, :, off:off + LANES] = cb.T
        n1 = (-cb).astype(BF16).astype(F32)
        n2 = (-cb - n1).astype(BF16).astype(F32)
        n3 = (-cb - n1 - n2).astype(BF16).astype(F32)
        aug = jnp.where(c_io < FB_LANE, 0.0, jnp.where(
            c_io < FB_LANE + N_HEADS, n1, jnp.where(
                c_io < FB_LANE + 2 * N_HEADS, pltpu.roll(n2, N_HEADS, 1), jnp.where(
                    c_io < FB_LANE + 3 * N_HEADS, pltpu.roll(n3, 2 * N_HEADS, 1), 0.0))))
        kaug_ref[kc, off:off + LANES, :] = aug.astype(BF16)
        for p in range(N_HEADS // 2):
            v_t = v_ref[0, rows, p * LANES:(p + 1) * LANES].astype(F32).T
            vt_ref[2 * p, kc, :, off:off + LANES] = jnp.where(r_io < HEAD_DIM, v_t, 1.0).astype(BF16)
            vt_ref[2 * p + 1, kc, :, off:off + LANES] = jnp.where(r_io < HEAD_DIM, 1.0, v_t).astype(BF16)

    row_q = lax.broadcasted_iota(jnp.int32, (LANES, TQ), 0)
    for h in range(N_HEADS):
        pick = jnp.where(row_q == FB_LANE + h, 1.0, jnp.where(
            row_q == FB_LANE + N_HEADS + h, 1.0, jnp.where(row_q == FB_LANE + 2 * N_HEADS + h, 1.0, 0.0)))
        qh_ref[h, LANES:2 * LANES, :] = pick.astype(BF16)

    def k_chunk(h, kc):
        k2 = k_ref[0, pl.ds(pl.multiple_of(kc * TK, TK), TK), (h // 2) * LANES:(h // 2 + 1) * LANES]
        return jnp.concatenate([k2, kaug_ref[kc]], axis=1)

    diag = lax.broadcasted_iota(jnp.int32, (TK, TQ), 0) <= lax.broadcasted_iota(jnp.int32, (TK, TQ), 1)
    step = functools.partial(_logit_chunk, k_chunk=k_chunk, addend=None, qh_ref=qh_ref, t_ref=t_ref)

    def query_block(i, carry):
        q0 = pl.multiple_of(i * TQ, TQ)
        _split_heads(q_ref, qh_ref, TQ, q0)
        cq = [crow_ref[i, FB_LANE + h:FB_LANE + h + 1, :] for h in range(N_HEADS)]
        ms = _chunk_loop(i, functools.partial(step, causal_mask=None), _attend_init(acc_ref, TQ))
        ms = step(i, ms, causal_mask=diag)
        shift = [cq[h] - (ms[h] + cq[h]) for h in range(N_HEADS)]
        _chunk_loop(i + 1, functools.partial(_value_chunk, vt_chunk=lambda h, kc: vt_ref[h, kc],
                                             shift=shift, t_ref=t_ref, acc_ref=acc_ref), 0)
        _attend_finish(acc_ref, o_ref, TQ, q0)
        return carry

    lax.fori_loop(0, S // TQ, query_block, 0)


def _bit_transpose(a):
    a = list(a)
    j, m = WORD // 2, 0x0000FFFF
    while j:
        for k in range(WORD):
            if k & j == 0:
                t = (lax.shift_right_logical(a[k], jnp.int32(j)) ^ a[k + j]) & np.int32(np.uint32(m))
                a[k + j] = a[k + j] ^ t
                a[k] = a[k] ^ lax.shift_left(t, jnp.int32(j))
        j //= 2
        m = (m ^ (m << j)) & 0xFFFFFFFF
    return a


def _dsa_kernel(qi_ref, qa_ref, sm_ref, o_ref,
                ki2_ref, ka2_ref, vtlo_ref, vthi_ref, scores_ref, planes_ref, qh_ref, acc_ref, t_ref,
                *, S, TQ, K):
    TK = TQ
    assert TK == WORD * 8

    planes_ref[...] = jnp.zeros(planes_ref.shape, jnp.int32)
    lane = lax.broadcasted_iota(jnp.int32, (LANES, LANES), 1)
    lo = lane < HEAD_DIM
    for r in range(S // LANES):
        c0 = sm_ref[0, r * LANES:(r + 1) * LANES, 0:LANES]
        c1 = sm_ref[0, r * LANES:(r + 1) * LANES, LANES:SMALL_W]
        c0r = pltpu.roll(c0, HEAD_DIM, 1)
        kc, off = (r * LANES) // TK, (r * LANES) % TK
        ka2_ref[kc, off:off + LANES, :] = jnp.where(lo, c0, c0r).astype(BF16)
        ki2_ref[kc, off:off + LANES, :] = jnp.where(lo, c1, pltpu.roll(c1, HEAD_DIM, 1)).astype(BF16)
        vtlo_ref[kc, :, off:off + LANES] = jnp.where(lo, c0r, 1.0).T.astype(BF16)
        vthi_ref[kc, :, off:off + LANES] = jnp.where(lo, 1.0, c0).T.astype(BF16)

    kpos_io = lax.broadcasted_iota(jnp.int32, (TK, TQ), 0)
    tri = jnp.where(lax.broadcasted_iota(jnp.int32, (TK, TK), 0) >= lax.broadcasted_iota(jnp.int32, (TK, TK), 1),
                    1.0, 0.0).astype(BF16)
    NR = (S // TK) * 8
    r_io = lax.broadcasted_iota(jnp.int32, (NR, TQ), 0)
    q_io = lax.broadcasted_iota(jnp.int32, (NR, TQ), 1)
    n_ok = jnp.where(q_io >= (r_io & 7), ((q_io - (r_io & 7)) >> 3) + 1, 0)
    live_diag = jnp.where(n_ok >= WORD, -1, lax.shift_left(1, jnp.minimum(n_ok, WORD - 1)) - 1)

    def query_block(i, carry):
        nk = i + 1
        q0 = pl.multiple_of(i * TQ, TQ)
        qpos = q0 + lax.broadcasted_iota(jnp.int32, (TK, TQ), 1)

        w_t = sm_ref[0, pl.ds(q0, TQ), LANES:SMALL_W].T
        _split_heads(qi_ref, qh_ref, TQ, q0)

        def score_chunk(kc, _):
            kch = ki2_ref[kc]
            acc = jnp.zeros((TK, TQ), F32)
            for h in range(N_HEADS):
                d = jnp.dot(kch, qh_ref[h], preferred_element_type=F32)
                acc = acc + w_t[WI_LANE + h:WI_LANE + h + 1, :] * jnp.maximum(d, 0.0)
            sc = jnp.where(kc * TK + kpos_io <= qpos, acc, -jnp.inf)
            scores_ref[kc] = sc.reshape(WORD, 8, TQ)
            bits = lax.bitcast_convert_type(sc, jnp.int32)
            k3 = (bits ^ ((bits >> 31) & np.int32(0x7FFFFFFF))).reshape(WORD, 8, TQ)
            planes = _bit_transpose([k3[j] for j in range(WORD)])
            planes[WORD - 1] = ~planes[WORD - 1]
            row0 = pl.multiple_of(kc * 8, 8)
            for b in range(WORD):
                planes_ref[b, pl.ds(row0, 8), :] = planes[b]
            return 0

        _chunk_loop(nk, score_chunk, 0)

        live0 = jnp.where((r_io >> 3) < i, -1, jnp.where((r_io >> 3) == i, live_diag, 0))

        def bit_body(it, carry):
            live, above, res = carry
            b = WORD - 1 - it
            ones = live & planes_ref[b]
            tot = above + jnp.sum(lax.population_count(ones), axis=0, keepdims=True)
            take = tot >= K
            res = jnp.where(take, res | lax.shift_left(jnp.int32(1), b), res)
            above = jnp.where(take, above, tot)
            live = jnp.where(take, ones, live ^ ones)
            return live, above, res

        zero_row = jnp.zeros((1, TQ), jnp.int32)
        _, _, res = lax.fori_loop(0, WORD, bit_body, (live0, zero_row, zero_row))
        guess = res ^ INT_MIN
        guess = lax.bitcast_convert_type(guess ^ ((guess >> 31) & np.int32(0x7FFFFFFF)), F32)
        guess = jnp.where(res == 0, -jnp.inf, guess)

        def census(x, neighbours):
            x8 = jnp.broadcast_to(x, (8, TQ))[None]

            def body(kc, c):
                sc = scores_ref[kc]
                gt, ge = sc > x8, sc >= x8
                out = [c[0] + jnp.sum(jnp.where(gt, 1.0, 0.0), axis=0), c[1] + jnp.sum(jnp.where(ge, 1.0, 0.0), axis=0)]
                if neighbours:
                    out += [jnp.minimum(c[2], jnp.min(jnp.where(gt, sc, jnp.inf), axis=0)),
                            jnp.maximum(c[3], jnp.max(jnp.where(ge, -jnp.inf, sc), axis=0))]
                return tuple(out)

            init = [jnp.zeros((8, TQ), F32)] * 2
            if neighbours:
                init += [jnp.full((8, TQ), jnp.inf, F32), jnp.full((8, TQ), -jnp.inf, F32)]
            c = lax.fori_loop(0, nk, body, tuple(init))
            red = [jnp.sum(c[0], axis=0, keepdims=True), jnp.sum(c[1], axis=0, keepdims=True)]
            if neighbours:
                red += [jnp.min(c[2], axis=0, keepdims=True), jnp.max(c[3], axis=0, keepdims=True)]
            return red

        def unsettled(n_gt, n_ge):
            return jnp.sum(jnp.where(n_gt >= K, 1.0, jnp.where(n_ge < K, 1.0, 0.0)))

        def walk(carry):
            x = carry[0]
            n_gt, n_ge, above_x, below_x = census(x, True)
            x = jnp.where(n_gt >= K, above_x, jnp.where(n_ge < K, below_x, x))
            return x, n_gt, unsettled(n_gt, n_ge)

        n_gt, n_ge = census(guess, False)
        thr, n_gt, _ = lax.while_loop(lambda c: c[2] > 0.0, walk, (guess, n_gt, unsettled(n_gt, n_ge)))
        need = K - n_gt

        _split_heads(qa_ref, qh_ref, TQ, q0)

        def sel_chunk(kc, carry):
            seen, ms = carry
            kk = scores_ref[kc].reshape(TK, TQ)
            eq = kk == thr
            pre = jnp.dot(tri, jnp.where(eq, 1.0, 0.0).astype(BF16), preferred_element_type=F32) + seen
            tie = jnp.where(eq, jnp.where(pre <= need, 0.0, NEG), NEG)
            b = jnp.where(kk > thr, 0.0, tie)
            b = jnp.where(kc * TK + kpos_io <= qpos, b, NEG)
            ms = _logit_chunk(kc, ms, k_chunk=lambda h, kc: ka2_ref[kc], addend=lambda h, kc: b,
                              qh_ref=qh_ref, t_ref=t_ref, causal_mask=None)
            return pre[TK - 1:TK, :], ms

        _, ms = _chunk_loop(nk, sel_chunk, (jnp.zeros((1, TQ), F32), _attend_init(acc_ref, TQ)))
        _chunk_loop(nk, functools.partial(
            _value_chunk, vt_chunk=lambda h, kc: (vtlo_ref if h % 2 == 0 else vthi_ref)[kc],
            shift=[-m for m in ms], t_ref=t_ref, acc_ref=acc_ref), 0)
        _attend_finish(acc_ref, o_ref, TQ, q0)
        return carry

    lax.fori_loop(0, S // TQ, query_block, 0)


def _out_kernel(x_ref, aa_ref, ga_ref, ab_ref, gb_ref, p_ref, gpre_ref, gpost_ref, gple_ref,
                wm_ref, wa_ref, wb_ref, wo_ref, wp_ref, wg_ref, o_ref, *, D):
    x = x_ref[...]
    hb = _rmsnorm(x, gpre_ref[...]).astype(BF16)
    mg = jax.nn.sigmoid(jnp.dot(hb, wm_ref[...], preferred_element_type=F32))

    def branch(a_ref, g_ref, w_ref):
        g = g_ref[...].astype(F32)
        u = (a_ref[...].astype(F32) * (g * jax.nn.sigmoid(g))).astype(BF16)
        return jnp.dot(u, w_ref[...], preferred_element_type=F32)

    merged = mg[:, :D] * branch(aa_ref, ga_ref, wa_ref) + mg[:, D:] * branch(ab_ref, gb_ref, wb_ref)
    out = jnp.dot(merged.astype(BF16), wo_ref[...], preferred_element_type=F32)
    x1 = x + _rmsnorm(out, gpost_ref[...])
    e = jnp.dot(p_ref[...].astype(BF16), wp_ref[...], preferred_element_type=F32)
    gate = jax.nn.sigmoid(jnp.dot(x1.astype(BF16), wg_ref[...], preferred_element_type=F32))
    o_ref[...] = x1 + _rmsnorm(gate * e, gple_ref[...])


def _const_spec(shape):
    return pl.BlockSpec(shape, lambda *_: (0,) * len(shape), pipeline_mode=pl.Buffered(1))


def _rope_tables(S):
    half = ROPE_DIM // 2
    freqs = ROPE_THETA ** (-jnp.arange(half, dtype=F32) / half)
    ang = jnp.arange(S).astype(F32)[:, None] * freqs[None, :]
    cos, sin = jnp.cos(ang), jnp.sin(ang)
    pad = HEAD_DIM - ROPE_DIM
    c64 = jnp.concatenate([cos, cos, jnp.ones((S, pad), F32)], axis=1)
    s1 = jnp.concatenate([-sin, jnp.zeros((S, HEAD_DIM - half), F32)], axis=1)
    s2 = jnp.concatenate([jnp.zeros((S, half), F32), sin, jnp.zeros((S, pad), F32)], axis=1)
    rep = LANES // HEAD_DIM
    return jnp.tile(c64, (1, rep)), jnp.tile(s1, (1, rep)), jnp.tile(s2, (1, rep))


def _layer(x, p, w_in, b_forget, w_branch_a, w_branch_b, w_merge, w_out,
           g_pre, g_post, w_ple, w_ple_gate, g_ple):
    B, S, D = x.shape
    N = B * S
    d_ple = p.shape[-1]
    TM = 1024
    TQ = 256
    assert S % TM == 0 and S % TQ == 0 and TQ % LANES == 0
    topk = min(TOPK_MAX, S // 4)
    idx_scale = (N_HEADS ** -0.5) * (HEAD_DIM ** -0.5)
    q_scale = LOG2E * HEAD_DIM ** -0.5

    W = WIDTH
    o_ka = W
    o_va = o_ka + HEAD_DIM
    o_ga = o_va + HEAD_DIM
    o_qi = o_ga + W
    o_ki = o_qi + W
    o_wi = o_ki + HEAD_DIM
    o_qb = o_wi + N_HEADS
    o_kb = o_qb + W
    o_vb = o_kb + W
    o_fb = o_vb + W
    o_gb = o_fb + N_HEADS
    assert o_gb + W == w_in.shape[1]
    cols = lambda o, n: w_in[:, o:o + n]
    w_big = jnp.concatenate([cols(0, W), cols(o_ga, W), cols(o_qi, W), cols(o_qb, W),
                             cols(o_kb, W), cols(o_vb, W), cols(o_gb, W)], axis=1).astype(BF16)
    w_small = jnp.concatenate([cols(o_ka, HEAD_DIM), cols(o_va, HEAD_DIM), cols(o_ki, HEAD_DIM),
                               cols(o_wi, N_HEADS), cols(o_fb, N_HEADS),
                               jnp.zeros((D, SMALL_W - 3 * HEAD_DIM - 2 * N_HEADS), w_in.dtype)],
                              axis=1).astype(BF16)
    rc, rs1, rs2 = _rope_tables(S)
    bf_pad = jnp.zeros((1, LANES), F32).at[0, FB_LANE:FB_LANE + N_HEADS].set(b_forget.astype(F32))

    params = functools.partial(pltpu.CompilerParams, vmem_limit_bytes=VMEM_LIMIT,
                               dimension_semantics=("arbitrary",))
    x2 = x.reshape(N, D)
    tok = lambda w: pl.BlockSpec((TM, w), lambda i: (i, 0))
    rope_spec = pl.BlockSpec((TM, LANES), lambda i: (i % (S // TM), 0))
    act = jax.ShapeDtypeStruct((N, W), BF16)

    qa, ga, qi, qb, kb, vb, gb, small = pl.pallas_call(
        functools.partial(_proj_kernel, idx_scale=idx_scale, q_scale=q_scale),
        grid=(N // TM,),
        in_specs=[tok(D), _const_spec((1, D)), _const_spec((D, 7 * W)), _const_spec((D, SMALL_W)),
                  rope_spec, rope_spec, rope_spec],
        out_specs=[tok(W)] * 7 + [tok(SMALL_W)],
        out_shape=[act] * 7 + [jax.ShapeDtypeStruct((N, SMALL_W), F32)],
        compiler_params=params(),
        name="proj",
    )(x2, g_pre.reshape(1, D), w_big, w_small, rc, rs1, rs2)

    r3 = lambda a: a.reshape(B, S, a.shape[-1])
    small3 = r3(small)
    seq = lambda w, j: pl.BlockSpec((1, S, w), lambda b: (b, 0, j))
    att = jax.ShapeDtypeStruct((B, S, W), BF16)
    head_scratch = [pltpu.VMEM((N_HEADS, LANES, TQ), BF16), pltpu.VMEM((N_HEADS, LANES, TQ), F32),
                    pltpu.VMEM((N_HEADS, S // TQ, TQ, TQ), F32)]

    att_b = pl.pallas_call(
        functools.partial(_fox_kernel, S=S, TQ=TQ),
        grid=(B,),
        in_specs=[seq(W, 0), seq(W, 0), seq(W, 0), seq(LANES, 1),
                  pl.BlockSpec((1, LANES), lambda b: (0, 0))],
        out_specs=seq(W, 0),
        out_shape=att,
        scratch_shapes=[pltpu.VMEM((S // TQ, LANES, TQ), F32), pltpu.VMEM((S // TQ, TQ, LANES), BF16),
                        pltpu.VMEM((N_HEADS, S // TQ, LANES, TQ), BF16),
                        pltpu.VMEM((N_HEADS, 2 * LANES, TQ), BF16)] + head_scratch[1:],
        compiler_params=params(),
        name="fox",
    )(r3(qb), r3(kb), r3(vb), small3, bf_pad)

    att_a = pl.pallas_call(
        functools.partial(_dsa_kernel, S=S, TQ=TQ, K=topk),
        grid=(B,),
        in_specs=[seq(W, 0), seq(W, 0), seq(SMALL_W, 0)],
        out_specs=seq(W, 0),
        out_shape=att,
        scratch_shapes=[pltpu.VMEM((S // TQ, TQ, LANES), BF16), pltpu.VMEM((S // TQ, TQ, LANES), BF16),
                        pltpu.VMEM((S // TQ, LANES, TQ), BF16), pltpu.VMEM((S // TQ, LANES, TQ), BF16),
                        pltpu.VMEM((S // TQ, TQ // 8, 8, TQ), F32),
                        pltpu.VMEM((WORD, (S // TQ) * 8, TQ), jnp.int32)] + head_scratch,
        compiler_params=params(),
        name="dsa",
    )(r3(qi), r3(qa), small3)

    bf = lambda w: w.astype(BF16)
    vec = lambda g: g.reshape(1, D)
    return pl.pallas_call(
        functools.partial(_out_kernel, D=D),
        grid=(N // TM,),
        in_specs=[tok(D), tok(W), tok(W), tok(W), tok(W), tok(d_ple),
                  _const_spec((1, D)), _const_spec((1, D)), _const_spec((1, D)),
                  _const_spec((D, 2 * D)), _const_spec((W, D)), _const_spec((W, D)),
                  _const_spec((D, D)), _const_spec((d_ple, D)), _const_spec((D, D))],
        out_specs=tok(D),
        out_shape=jax.ShapeDtypeStruct((N, D), x.dtype),
        compiler_params=params(),
        name="out",
    )(x2, att_a.reshape(N, W), ga, att_b.reshape(N, W), gb, p.reshape(N, d_ple),
      vec(g_pre), vec(g_post), vec(g_ple),
      bf(w_merge), bf(w_branch_a), bf(w_branch_b), bf(w_out), bf(w_ple), bf(w_ple_gate)).reshape(B, S, D)


def kernel(x, p, w_in, b_forget, w_branch_a, w_branch_b, w_merge, w_out, g_pre, g_post, w_ple, w_ple_gate, g_ple):
    for i in range(p.shape[0]):
        x = _layer(x, p[i], w_in[i], b_forget[i], w_branch_a[i], w_branch_b[i], w_merge[i], w_out[i],
                   g_pre[i], g_post[i], w_ple[i], w_ple_gate[i], g_ple[i])
    return x
```

```python
import functools

import numpy as np
import jax
import jax.numpy as jnp
from jax import lax
from jax.experimental import pallas as pl
from jax.experimental.pallas import tpu as pltpu

HEAD_DIM = 64
ROPE_DIM = HEAD_DIM // 4
ROPE_THETA = 500000.0
N_HEADS = 8
WIDTH = N_HEADS * HEAD_DIM
TOPK_MAX = 256
EPS = 1e-6
NEG = -1e30

LANES = 128
SMALL_W = 2 * LANES
WI_LANE = HEAD_DIM
FB_LANE = HEAD_DIM + N_HEADS
INT_MIN = np.int32(-2**31)
WORD = 32
LOG2E = 1.4426950408889634
VMEM_LIMIT = 56 * 1024 * 1024
T_PAD_ROWS = 8

F32 = jnp.float32
BF16 = jnp.bfloat16


def _rmsnorm(x, g):
    return x * lax.rsqrt(jnp.mean(x * x, axis=-1, keepdims=True) + EPS) * g


def _proj_kernel(x_ref, g_ref, wbig_ref, wsmall_ref, rc_ref, rs1_ref, rs2_ref,
                 qa_ref, ga_ref, qi_ref, qb_ref, kb_ref, vb_ref, gb_ref, small_ref,
                 *, idx_scale, q_scale):
    hb = _rmsnorm(x_ref[...], g_ref[...]).astype(BF16)
    rc, rs1, rs2 = rc_ref[...], rs1_ref[...], rs2_ref[...]

    def rope(y, c, s1, s2):
        return y * c + pltpu.roll(y, LANES - ROPE_DIM // 2, 1) * s1 + pltpu.roll(y, ROPE_DIM // 2, 1) * s2

    def proj(j):
        return jnp.dot(hb, wbig_ref[:, j * WIDTH:(j + 1) * WIDTH], preferred_element_type=F32)

    y = proj(0)
    for c in range(WIDTH // LANES):
        sl = slice(c * LANES, (c + 1) * LANES)
        qa_ref[:, sl] = (rope(y[:, sl], rc, rs1, rs2) * q_scale).astype(BF16)
    ga_ref[...] = proj(1).astype(BF16)
    y = proj(2)
    for c in range(WIDTH // LANES):
        sl = slice(c * LANES, (c + 1) * LANES)
        qi_ref[:, sl] = rope(y[:, sl], rc, rs1, rs2).astype(BF16)
    qb_ref[...] = (proj(3) * q_scale).astype(BF16)
    kb_ref[...] = proj(4).astype(BF16)
    vb_ref[...] = proj(5).astype(BF16)
    gb_ref[...] = proj(6).astype(BF16)

    ys = jnp.dot(hb, wsmall_ref[...], preferred_element_type=F32)
    lane = lax.broadcasted_iota(jnp.int32, rc.shape, 1)
    lo = lane < HEAD_DIM
    c_lo = jnp.where(lo, rc, 1.0)
    s1_lo = jnp.where(lo, rs1, 0.0)
    s2_lo = jnp.where(lo, rs2, 0.0)
    small_ref[:, 0:LANES] = rope(ys[:, 0:LANES], c_lo, s1_lo, s2_lo)
    wi_scale = jnp.where(lane < WI_LANE, 1.0, jnp.where(lane < FB_LANE, idx_scale, 1.0))
    small_ref[:, LANES:SMALL_W] = rope(ys[:, LANES:SMALL_W], c_lo, s1_lo, s2_lo) * wi_scale


def _split_heads(src_ref, qh_ref, TQ, q0):
    row = lax.broadcasted_iota(jnp.int32, (LANES, TQ), 0)
    for p in range(N_HEADS // 2):
        q_t = src_ref[0, pl.ds(q0, TQ), p * LANES:(p + 1) * LANES].astype(F32).T
        qh_ref[2 * p, 0:LANES, :] = jnp.where(row < HEAD_DIM, q_t, 0.0).astype(BF16)
        qh_ref[2 * p + 1, 0:LANES, :] = jnp.where(row < HEAD_DIM, 0.0, q_t).astype(BF16)


def _logit_chunk(kc, ms, *, k_chunk, addend, qh_ref, t_ref, causal_mask):
    new_ms = []
    for h in range(N_HEADS):
        t = jnp.dot(k_chunk(h, kc), qh_ref[h], preferred_element_type=F32)
        if addend is not None:
            t = t + addend(h, kc)
        if causal_mask is not None:
            t = jnp.where(causal_mask, t, NEG)
        t_ref[h, kc, 0:t.shape[0], :] = t
        new_ms.append(jnp.maximum(ms[h], jnp.max(t, axis=0, keepdims=True)))
    return tuple(new_ms)


def _value_chunk(kc, carry, *, vt_chunk, shift, t_ref, acc_ref):
    for h in range(N_HEADS):
        pm = jnp.exp2((t_ref[h, kc, 0:t_ref.shape[3], :] + shift[h]).astype(BF16))
        acc_ref[h] += jnp.dot(vt_chunk(h, kc), pm, preferred_element_type=F32)
    return carry


def _chunk_loop(n, body, init):
    def run(lo, trips, width, carry):
        def trip(j, c):
            for u in range(width):
                c = body(lo + width * j + u, c)
            return c
        return lax.fori_loop(0, trips, trip, carry)

    quads = lax.shift_right_logical(n, 2)
    carry = run(0, quads, 4, init)
    pair = lax.shift_right_logical(n, 1) & 1
    carry = run(4 * quads, pair, 2, carry)
    return run(4 * quads + 2 * pair, n & 1, 1, carry)


def _attend_init(acc_ref, TQ):
    acc_ref[...] = jnp.zeros(acc_ref.shape, F32)
    return tuple(jnp.full((1, TQ), -jnp.inf, F32) for _ in range(N_HEADS))


def _attend_finish(acc_ref, o_ref, TQ, q0):
    row = lax.broadcasted_iota(jnp.int32, (LANES, TQ), 0)
    for p in range(N_HEADS // 2):
        a0, a1 = acc_ref[2 * p], acc_ref[2 * p + 1]
        o_t = jnp.where(row < HEAD_DIM, a0 / a0[HEAD_DIM:HEAD_DIM + 1, :], a1 / a1[0:1, :])
        o_ref[0, pl.ds(q0, TQ), p * LANES:(p + 1) * LANES] = o_t.T.astype(BF16)


def _fox_kernel(q_ref, k_ref, v_ref, sm_ref, bf_ref, o_ref,
                crow_ref, kaug_ref, vt_ref, qh_ref, acc_ref, t_ref, *, S, TQ):
    TK = TQ

    r_io = lax.broadcasted_iota(jnp.int32, (LANES, LANES), 0)
    c_io = lax.broadcasted_iota(jnp.int32, (LANES, LANES), 1)
    tri = jnp.where(r_io >= c_io, 1.0, 0.0).astype(BF16)
    carry = jnp.zeros((1, LANES), F32)
    for r in range(S // LANES):
        rows = slice(r * LANES, (r + 1) * LANES)
        kc, off = (r * LANES) // TK, (r * LANES) % TK
        z = sm_ref[0, rows, :] + bf_ref[...]
        lf = -(jnp.maximum(-z, 0.0) + jnp.log1p(jnp.exp(-jnp.abs(z)))) * LOG2E
        a1 = lf.astype(BF16)
        r1 = lf - a1.astype(F32)
        a2 = r1.astype(BF16)
        a3 = (r1 - a2.astype(F32)).astype(BF16)
        cb = (jnp.dot(tri, a1, preferred_element_type=F32)
              + jnp.dot(tri, a2, preferred_element_type=F32)
              + jnp.dot(tri, a3, preferred_element_type=F32)) + carry
        carry = cb[LANES - 1:LANES, :]
        crow_ref[kc, :, off:off + LANES] = cb.T
        n1 = (-cb).astype(BF16).astype(F32)
        n2 = (-cb - n1).astype(BF16).astype(F32)
        n3 = (-cb - n1 - n2).astype(BF16).astype(F32)
        aug = jnp.where(c_io < FB_LANE, 0.0, jnp.where(
            c_io < FB_LANE + N_HEADS, n1, jnp.where(
                c_io < FB_LANE + 2 * N_HEADS, pltpu.roll(n2, N_HEADS, 1), jnp.where(
                    c_io < FB_LANE + 3 * N_HEADS, pltpu.roll(n3, 2 * N_HEADS, 1), 0.0))))
        kaug_ref[kc, off:off + LANES, :] = aug.astype(BF16)
        for p in range(N_HEADS // 2):
            v_t = v_ref[0, rows, p * LANES:(p + 1) * LANES].astype(F32).T
            vt_ref[2 * p, kc, :, off:off + LANES] = jnp.where(r_io < HEAD_DIM, v_t, 1.0).astype(BF16)
            vt_ref[2 * p + 1, kc, :, off:off + LANES] = jnp.where(r_io < HEAD_DIM, 1.0, v_t).astype(BF16)

    row_q = lax.broadcasted_iota(jnp.int32, (LANES, TQ), 0)
    for h in range(N_HEADS):
        pick = jnp.where(row_q == FB_LANE + h, 1.0, jnp.where(
            row_q == FB_LANE + N_HEADS + h, 1.0, jnp.where(row_q == FB_LANE + 2 * N_HEADS + h, 1.0, 0.0)))
        qh_ref[h, LANES:2 * LANES, :] = pick.astype(BF16)

    def k_chunk(h, kc):
        k2 = k_ref[0, pl.ds(pl.multiple_of(kc * TK, TK), TK), (h // 2) * LANES:(h // 2 + 1) * LANES]
        return jnp.concatenate([k2, kaug_ref[kc]], axis=1)

    diag = lax.broadcasted_iota(jnp.int32, (TK, TQ), 0) <= lax.broadcasted_iota(jnp.int32, (TK, TQ), 1)
    step = functools.partial(_logit_chunk, k_chunk=k_chunk, addend=None, qh_ref=qh_ref, t_ref=t_ref)

    def query_block(i, carry):
        q0 = pl.multiple_of(i * TQ, TQ)
        _split_heads(q_ref, qh_ref, TQ, q0)
        cq = [crow_ref[i, FB_LANE + h:FB_LANE + h + 1, :] for h in range(N_HEADS)]
        ms = _chunk_loop(i, functools.partial(step, causal_mask=None), _attend_init(acc_ref, TQ))
        ms = step(i, ms, causal_mask=diag)
        shift = [cq[h] - (ms[h] + cq[h]) for h in range(N_HEADS)]
        _chunk_loop(i + 1, functools.partial(_value_chunk, vt_chunk=lambda h, kc: vt_ref[h, kc],
                                             shift=shift, t_ref=t_ref, acc_ref=acc_ref), 0)
        _attend_finish(acc_ref, o_ref, TQ, q0)
        return carry

    lax.fori_loop(0, S // TQ, query_block, 0)


def _bit_transpose(a):
    a = list(a)
    j, m = WORD // 2, 0x0000FFFF
    while j:
        for k in range(WORD):
            if k & j == 0:
                t = (lax.shift_right_logical(a[k], jnp.int32(j)) ^ a[k + j]) & np.int32(np.uint32(m))
                a[k + j] = a[k + j] ^ t
                a[k] = a[k] ^ lax.shift_left(t, jnp.int32(j))
        j //= 2
        m = (m ^ (m << j)) & 0xFFFFFFFF
    return a


def _dsa_kernel(qi_ref, qa_ref, sm_ref, o_ref,
                ki2_ref, ka2_ref, vtlo_ref, vthi_ref, scores_ref, planes_ref, qh_ref, acc_ref, t_ref,
                *, S, TQ, K):
    TK = TQ
    assert TK == WORD * 8

    planes_ref[...] = jnp.zeros(planes_ref.shape, jnp.int32)
    lane = lax.broadcasted_iota(jnp.int32, (LANES, LANES), 1)
    lo = lane < HEAD_DIM
    for r in range(S // LANES):
        c0 = sm_ref[0, r * LANES:(r + 1) * LANES, 0:LANES]
        c1 = sm_ref[0, r * LANES:(r + 1) * LANES, LANES:SMALL_W]
        c0r = pltpu.roll(c0, HEAD_DIM, 1)
        kc, off = (r * LANES) // TK, (r * LANES) % TK
        ka2_ref[kc, off:off + LANES, :] = jnp.where(lo, c0, c0r).astype(BF16)
        ki2_ref[kc, off:off + LANES, :] = jnp.where(lo, c1, pltpu.roll(c1, HEAD_DIM, 1)).astype(BF16)
        vtlo_ref[kc, :, off:off + LANES] = jnp.where(lo, c0r, 1.0).T.astype(BF16)
        vthi_ref[kc, :, off:off + LANES] = jnp.where(lo, 1.0, c0).T.astype(BF16)

    kpos_io = lax.broadcasted_iota(jnp.int32, (TK, TQ), 0)
    tri = jnp.where(lax.broadcasted_iota(jnp.int32, (TK, TK), 0) >= lax.broadcasted_iota(jnp.int32, (TK, TK), 1),
                    1.0, 0.0).astype(BF16)
    NR = (S // TK) * 8
    r_io = lax.broadcasted_iota(jnp.int32, (NR, TQ), 0)
    q_io = lax.broadcasted_iota(jnp.int32, (NR, TQ), 1)
    n_ok = jnp.where(q_io >= (r_io & 7), ((q_io - (r_io & 7)) >> 3) + 1, 0)
    live_diag = jnp.where(n_ok >= WORD, -1, lax.shift_left(1, jnp.minimum(n_ok, WORD - 1)) - 1)

    def query_block(i, carry):
        nk = i + 1
        q0 = pl.multiple_of(i * TQ, TQ)
        qpos = q0 + lax.broadcasted_iota(jnp.int32, (TK, TQ), 1)

        w_t = sm_ref[0, pl.ds(q0, TQ), LANES:SMALL_W].T
        _split_heads(qi_ref, qh_ref, TQ, q0)

        def score_chunk(kc, _):
            kch = ki2_ref[kc]
            acc = jnp.zeros((TK, TQ), F32)
            for h in range(N_HEADS):
                d = jnp.dot(kch, qh_ref[h], preferred_element_type=F32)
                acc = acc + w_t[WI_LANE + h:WI_LANE + h + 1, :] * jnp.maximum(d, 0.0)
            sc = jnp.where(kc * TK + kpos_io <= qpos, acc, -jnp.inf)
            scores_ref[kc] = sc.reshape(WORD, 8, TQ)
            bits = lax.bitcast_convert_type(sc, jnp.int32)
            k3 = (bits ^ ((bits >> 31) & np.int32(0x7FFFFFFF))).reshape(WORD, 8, TQ)
            planes = _bit_transpose([k3[j] for j in range(WORD)])
            planes[WORD - 1] = ~planes[WORD - 1]
            row0 = pl.multiple_of(kc * 8, 8)
            for b in range(WORD):
                planes_ref[b, pl.ds(row0, 8), :] = planes[b]
            return 0

        _chunk_loop(nk, score_chunk, 0)

        live0 = jnp.where((r_io >> 3) < i, -1, jnp.where((r_io >> 3) == i, live_diag, 0))

        def bit_body(it, carry):
            live, above, res = carry
            b = WORD - 1 - it
            ones = live & planes_ref[b]
            tot = above + jnp.sum(lax.population_count(ones), axis=0, keepdims=True)
            take = tot >= K
            res = jnp.where(take, res | lax.shift_left(jnp.int32(1), b), res)
            above = jnp.where(take, above, tot)
            live = jnp.where(take, ones, live ^ ones)
            return live, above, res

        zero_row = jnp.zeros((1, TQ), jnp.int32)
        _, _, res = lax.fori_loop(0, WORD, bit_body, (live0, zero_row, zero_row))
        guess = res ^ INT_MIN
        guess = lax.bitcast_convert_type(guess ^ ((guess >> 31) & np.int32(0x7FFFFFFF)), F32)
        guess = jnp.where(res == 0, -jnp.inf, guess)

        def census(x, neighbours):
            x8 = jnp.broadcast_to(x, (8, TQ))[None]

            def body(kc, c):
                sc = scores_ref[kc]
                gt, ge = sc > x8, sc >= x8
                out = [c[0] + jnp.sum(jnp.where(gt, 1.0, 0.0), axis=0), c[1] + jnp.sum(jnp.where(ge, 1.0, 0.0), axis=0)]
                if neighbours:
                    out += [jnp.minimum(c[2], jnp.min(jnp.where(gt, sc, jnp.inf), axis=0)),
                            jnp.maximum(c[3], jnp.max(jnp.where(ge, -jnp.inf, sc), axis=0))]
                return tuple(out)

            init = [jnp.zeros((8, TQ), F32)] * 2
            if neighbours:
                init += [jnp.full((8, TQ), jnp.inf, F32), jnp.full((8, TQ), -jnp.inf, F32)]
            c = lax.fori_loop(0, nk, body, tuple(init))
            red = [jnp.sum(c[0], axis=0, keepdims=True), jnp.sum(c[1], axis=0, keepdims=True)]
            if neighbours:
                red += [jnp.min(c[2], axis=0, keepdims=True), jnp.max(c[3], axis=0, keepdims=True)]
            return red

        def unsettled(n_gt, n_ge):
            return jnp.sum(jnp.where(n_gt >= K, 1.0, jnp.where(n_ge < K, 1.0, 0.0)))

        def walk(carry):
            x = carry[0]
            n_gt, n_ge, above_x, below_x = census(x, True)
            x = jnp.where(n_gt >= K, above_x, jnp.where(n_ge < K, below_x, x))
            return x, n_gt, unsettled(n_gt, n_ge)

        n_gt, n_ge = census(guess, False)
        thr, n_gt, _ = lax.while_loop(lambda c: c[2] > 0.0, walk, (guess, n_gt, unsettled(n_gt, n_ge)))
        need = K - n_gt

        _split_heads(qa_ref, qh_ref, TQ, q0)

        def sel_chunk(kc, carry):
            seen, ms = carry
            kk = scores_ref[kc].reshape(TK, TQ)
            eq = kk == thr
            pre = jnp.dot(tri, jnp.where(eq, 1.0, 0.0).astype(BF16), preferred_element_type=F32) + seen
            tie = jnp.where(eq, jnp.where(pre <= need, 0.0, NEG), NEG)
            b = jnp.where(kk > thr, 0.0, tie)
            b = jnp.where(kc * TK + kpos_io <= qpos, b, NEG)
            ms = _logit_chunk(kc, ms, k_chunk=lambda h, kc: ka2_ref[kc], addend=lambda h, kc: b,
                              qh_ref=qh_ref, t_ref=t_ref, causal_mask=None)
            return pre[TK - 1:TK, :], ms

        _, ms = _chunk_loop(nk, sel_chunk, (jnp.zeros((1, TQ), F32), _attend_init(acc_ref, TQ)))
        _chunk_loop(nk, functools.partial(
            _value_chunk, vt_chunk=lambda h, kc: (vtlo_ref if h % 2 == 0 else vthi_ref)[kc],
            shift=[-m for m in ms], t_ref=t_ref, acc_ref=acc_ref), 0)
        _attend_finish(acc_ref, o_ref, TQ, q0)
        return carry

    lax.fori_loop(0, S // TQ, query_block, 0)


def _out_kernel(x_ref, aa_ref, ga_ref, ab_ref, gb_ref, p_ref, gpre_ref, gpost_ref, gple_ref,
                wm_ref, wa_ref, wb_ref, wo_ref, wp_ref, wg_ref, o_ref, *, D):
    x = x_ref[...]
    hb = _rmsnorm(x, gpre_ref[...]).astype(BF16)
    mg = jax.nn.sigmoid(jnp.dot(hb, wm_ref[...], preferred_element_type=F32))

    def branch(a_ref, g_ref, w_ref):
        g = g_ref[...].astype(F32)
        u = (a_ref[...].astype(F32) * (g * jax.nn.sigmoid(g))).astype(BF16)
        return jnp.dot(u, w_ref[...], preferred_element_type=F32)

    merged = mg[:, :D] * branch(aa_ref, ga_ref, wa_ref) + mg[:, D:] * branch(ab_ref, gb_ref, wb_ref)
    out = jnp.dot(merged.astype(BF16), wo_ref[...], preferred_element_type=F32)
    x1 = x + _rmsnorm(out, gpost_ref[...])
    e = jnp.dot(p_ref[...].astype(BF16), wp_ref[...], preferred_element_type=F32)
    gate = jax.nn.sigmoid(jnp.dot(x1.astype(BF16), wg_ref[...], preferred_element_type=F32))
    o_ref[...] = x1 + _rmsnorm(gate * e, gple_ref[...])


def _const_spec(shape):
    return pl.BlockSpec(shape, lambda *_: (0,) * len(shape), pipeline_mode=pl.Buffered(1))


def _rope_tables(S):
    half = ROPE_DIM // 2
    freqs = ROPE_THETA ** (-jnp.arange(half, dtype=F32) / half)
    ang = jnp.arange(S).astype(F32)[:, None] * freqs[None, :]
    cos, sin = jnp.cos(ang), jnp.sin(ang)
    pad = HEAD_DIM - ROPE_DIM
    c64 = jnp.concatenate([cos, cos, jnp.ones((S, pad), F32)], axis=1)
    s1 = jnp.concatenate([-sin, jnp.zeros((S, HEAD_DIM - half), F32)], axis=1)
    s2 = jnp.concatenate([jnp.zeros((S, half), F32), sin, jnp.zeros((S, pad), F32)], axis=1)
    rep = LANES // HEAD_DIM
    return jnp.tile(c64, (1, rep)), jnp.tile(s1, (1, rep)), jnp.tile(s2, (1, rep))


def _layer(x, p, w_in, b_forget, w_branch_a, w_branch_b, w_merge, w_out,
           g_pre, g_post, w_ple, w_ple_gate, g_ple):
    B, S, D = x.shape
    N = B * S
    d_ple = p.shape[-1]
    TM = 1024
    TQ = 256
    assert S % TM == 0 and S % TQ == 0 and TQ % LANES == 0
    topk = min(TOPK_MAX, S // 4)
    idx_scale = (N_HEADS ** -0.5) * (HEAD_DIM ** -0.5)
    q_scale = LOG2E * HEAD_DIM ** -0.5

    W = WIDTH
    o_ka = W
    o_va = o_ka + HEAD_DIM
    o_ga = o_va + HEAD_DIM
    o_qi = o_ga + W
    o_ki = o_qi + W
    o_wi = o_ki + HEAD_DIM
    o_qb = o_wi + N_HEADS
    o_kb = o_qb + W
    o_vb = o_kb + W
    o_fb = o_vb + W
    o_gb = o_fb + N_HEADS
    assert o_gb + W == w_in.shape[1]
    cols = lambda o, n: w_in[:, o:o + n]
    w_big = jnp.concatenate([cols(0, W), cols(o_ga, W), cols(o_qi, W), cols(o_qb, W),
                             cols(o_kb, W), cols(o_vb, W), cols(o_gb, W)], axis=1).astype(BF16)
    w_small = jnp.concatenate([cols(o_ka, HEAD_DIM), cols(o_va, HEAD_DIM), cols(o_ki, HEAD_DIM),
                               cols(o_wi, N_HEADS), cols(o_fb, N_HEADS),
                               jnp.zeros((D, SMALL_W - 3 * HEAD_DIM - 2 * N_HEADS), w_in.dtype)],
                              axis=1).astype(BF16)
    rc, rs1, rs2 = _rope_tables(S)
    bf_pad = jnp.zeros((1, LANES), F32).at[0, FB_LANE:FB_LANE + N_HEADS].set(b_forget.astype(F32))

    params = functools.partial(pltpu.CompilerParams, vmem_limit_bytes=VMEM_LIMIT,
                               dimension_semantics=("arbitrary",))
    x2 = x.reshape(N, D)
    tok = lambda w: pl.BlockSpec((TM, w), lambda i: (i, 0))
    rope_spec = pl.BlockSpec((TM, LANES), lambda i: (i % (S // TM), 0))
    act = jax.ShapeDtypeStruct((N, W), BF16)

    qa, ga, qi, qb, kb, vb, gb, small = pl.pallas_call(
        functools.partial(_proj_kernel, idx_scale=idx_scale, q_scale=q_scale),
        grid=(N // TM,),
        in_specs=[tok(D), _const_spec((1, D)), _const_spec((D, 7 * W)), _const_spec((D, SMALL_W)),
                  rope_spec, rope_spec, rope_spec],
        out_specs=[tok(W)] * 7 + [tok(SMALL_W)],
        out_shape=[act] * 7 + [jax.ShapeDtypeStruct((N, SMALL_W), F32)],
        compiler_params=params(),
        name="proj",
    )(x2, g_pre.reshape(1, D), w_big, w_small, rc, rs1, rs2)

    r3 = lambda a: a.reshape(B, S, a.shape[-1])
    small3 = r3(small)
    seq = lambda w, j: pl.BlockSpec((1, S, w), lambda b: (b, 0, j))
    att = jax.ShapeDtypeStruct((B, S, W), BF16)
    head_scratch = [pltpu.VMEM((N_HEADS, LANES, TQ), BF16), pltpu.VMEM((N_HEADS, LANES, TQ), F32),
                    pltpu.VMEM((N_HEADS, S // TQ, TQ + T_PAD_ROWS, TQ), F32)]

    att_b = pl.pallas_call(
        functools.partial(_fox_kernel, S=S, TQ=TQ),
        grid=(B,),
        in_specs=[seq(W, 0), seq(W, 0), seq(W, 0), seq(LANES, 1),
                  pl.BlockSpec((1, LANES), lambda b: (0, 0))],
        out_specs=seq(W, 0),
        out_shape=att,
        scratch_shapes=[pltpu.VMEM((S // TQ, LANES, TQ), F32), pltpu.VMEM((S // TQ, TQ, LANES), BF16),
                        pltpu.VMEM((N_HEADS, S // TQ, LANES, TQ), BF16),
                        pltpu.VMEM((N_HEADS, 2 * LANES, TQ), BF16)] + head_scratch[1:],
        compiler_params=params(),
        name="fox",
    )(r3(qb), r3(kb), r3(vb), small3, bf_pad)

    att_a = pl.pallas_call(
        functools.partial(_dsa_kernel, S=S, TQ=TQ, K=topk),
        grid=(B,),
        in_specs=[seq(W, 0), seq(W, 0), seq(SMALL_W, 0)],
        out_specs=seq(W, 0),
        out_shape=att,
        scratch_shapes=[pltpu.VMEM((S // TQ, TQ, LANES), BF16), pltpu.VMEM((S // TQ, TQ, LANES), BF16),
                        pltpu.VMEM((S // TQ, LANES, TQ), BF16), pltpu.VMEM((S // TQ, LANES, TQ), BF16),
                        pltpu.VMEM((S // TQ, TQ // 8, 8, TQ), F32),
                        pltpu.VMEM((WORD, (S // TQ) * 8, TQ), jnp.int32)] + head_scratch,
        compiler_params=params(),
        name="dsa",
    )(r3(qi), r3(qa), small3)

    bf = lambda w: w.astype(BF16)
    vec = lambda g: g.reshape(1, D)
    return pl.pallas_call(
        functools.partial(_out_kernel, D=D),
        grid=(N // TM,),
        in_specs=[tok(D), tok(W), tok(W), tok(W), tok(W), tok(d_ple),
                  _const_spec((1, D)), _const_spec((1, D)), _const_spec((1, D)),
                  _const_spec((D, 2 * D)), _const_spec((W, D)), _const_spec((W, D)),
                  _const_spec((D, D)), _const_spec((d_ple, D)), _const_spec((D, D))],
        out_specs=tok(D),
        out_shape=jax.ShapeDtypeStruct((N, D), x.dtype),
        compiler_params=params(),
        name="out",
    )(x2, att_a.reshape(N, W), ga, att_b.reshape(N, W), gb, p.reshape(N, d_ple),
      vec(g_pre), vec(g_post), vec(g_ple),
      bf(w_merge), bf(w_branch_a), bf(w_branch_b), bf(w_out), bf(w_ple), bf(w_ple_gate)).reshape(B, S, D)


def kernel(x, p, w_in, b_forget, w_branch_a, w_branch_b, w_merge, w_out, g_pre, g_post, w_ple, w_ple_gate, g_ple):
    for i in range(p.shape[0]):
        x = _layer(x, p[i], w_in[i], b_forget[i], w_branch_a[i], w_branch_b[i], w_merge[i], w_out[i],
                   g_pre[i], g_post[i], w_ple[i], w_ple_gate[i], g_ple[i])
    return x
```
